```python
import math
import jax, jax.numpy as jnp
from jax import lax
import numpy as np

D_MODEL = 1024
BATCH = 8
SEQ = 8192
DEPTH = 4

N_META = 16
BLOCK = 128
PAD_FRONT = BLOCK - N_META
HEAD_DIM = 64
SB_HEADS = 4
DIFF_HEADS = 4
CONV_GROUPS = 8
SB_W = SB_HEADS * HEAD_DIM
DIFF_W = DIFF_HEADS * HEAD_DIM
CONV_W = CONV_GROUPS * HEAD_DIM
MIX_W = SB_W + DIFF_W + CONV_W
IN_COLS = 3 * SB_W + 3 * DIFF_W + 3 * CONV_W
DIFF_SUB = HEAD_DIM // 2
ROPE_DIM = DIFF_SUB // 4
ROPE_THETA = 500000.0
CONV_K = 3
N_EXPERTS = 16
N_GROUPS = 4
EXPERTS_PER_GROUP = N_EXPERTS // N_GROUPS
TOP_K = 2
D_FF_EXPERT = 128
ALPHA = (2.0 * DEPTH) ** 0.25
BETA_INIT = (8.0 * DEPTH) ** -0.25
LN_EPS = 1e-5
NEG_BIG = -1e30

kernel_name = "hybrid_sb_diff_conv_grouped_moe"


def layer_norm(x, g, b):
    xf = x.astype(jnp.float32)
    mu = jnp.mean(xf, axis=-1, keepdims=True)
    xc = xf - mu
    var = jnp.mean(xc * xc, axis=-1, keepdims=True)
    y = xc * lax.rsqrt(var + LN_EPS) * g.astype(jnp.float32) + b.astype(jnp.float32)
    return y.astype(x.dtype)


def rms_norm(x, g):
    xf = x.astype(jnp.float32)
    y = xf * lax.rsqrt(jnp.mean(xf * xf, axis=-1, keepdims=True) + LN_EPS) * g.astype(jnp.float32)
    return y.astype(x.dtype)


def rope_tables(length, dtype):
    pos = (jnp.arange(length) - PAD_FRONT).astype(jnp.float32)
    inv = jnp.power(jnp.float32(ROPE_THETA), -jnp.arange(0, ROPE_DIM, 2, dtype=jnp.float32) / ROPE_DIM)
    ang = pos[:, None] * inv[None, :]
    return jnp.cos(ang).astype(dtype), jnp.sin(ang).astype(dtype)


def partial_rope(t, cos, sin):
    half = ROPE_DIM // 2
    x1, x2, rest = t[..., :half], t[..., half:ROPE_DIM], t[..., ROPE_DIM:]
    return jnp.concatenate([x1 * cos - x2 * sin, x1 * sin + x2 * cos, rest], axis=-1)


def _heads(t, n):
    b, l, _ = t.shape
    return t.reshape(b, l, n, -1).transpose(0, 2, 1, 3)


def _unheads(t):
    b, h, l, d = t.shape
    return t.transpose(0, 2, 1, 3).reshape(b, l, h * d)


def _strict_lower(n):
    return (jnp.arange(n)[:, None] > jnp.arange(n)[None, :]).astype(jnp.float32)


def stick_breaking_attention(q, k, v):
    b, h, length, _ = q.shape
    nb = length // BLOCK
    scale = HEAD_DIM ** -0.5
    tri_in = _strict_lower(BLOCK)
    outs = []
    for i in range(nb):
        nk = i + 1
        lk = nk * BLOCK
        qb = q[:, :, i * BLOCK:(i + 1) * BLOCK]
        kp, vp = k[:, :, :lk], v[:, :, :lk]
        z = jnp.einsum('bhqd,bhkd->bhqk', qb, kp).astype(jnp.float32) * scale
        q_idx = i * BLOCK + jnp.arange(BLOCK)
        key_idx = jnp.arange(lk)
        m = (key_idx[None, :] < q_idx[:, None]) & (key_idx[None, :] >= PAD_FRONT)
        log_beta = jax.nn.log_sigmoid(z)
        log_rem = jnp.where(m, log_beta - z, 0.0).reshape(b, h, BLOCK, nk, BLOCK)
        intra = jnp.einsum('bhqnj,js->bhqns', log_rem, tri_in)
        later = jnp.einsum('bhqn,nm->bhqm', log_rem.sum(-1), _strict_lower(nk))
        after = (intra + later[..., None]).reshape(b, h, BLOCK, lk)
        w = jnp.where(m, jnp.exp(log_beta + after), 0.0)
        outs.append(jnp.einsum('bhqk,bhkd->bhqd', w.astype(v.dtype), vp))
    return jnp.concatenate(outs, axis=2)


def differential_attention(q, k, v, lq1, lk1, lq2, lk2, norm_g, lam_init, cos, sin):
    b, length, _ = q.shape
    q = partial_rope(q.reshape(b, length, DIFF_HEADS, 2, DIFF_SUB).transpose(0, 2, 3, 1, 4), cos, sin)
    k = partial_rope(k.reshape(b, length, DIFF_HEADS, 2, DIFF_SUB).transpose(0, 2, 3, 1, 4), cos, sin)
    q1, q2 = q[:, :, 0], q[:, :, 1]
    k1, k2 = k[:, :, 0], k[:, :, 1]
    v = _heads(v, DIFF_HEADS)
    f32 = jnp.float32
    lam = (jnp.exp(jnp.sum(lq1.astype(f32) * lk1.astype(f32)))
           - jnp.exp(jnp.sum(lq2.astype(f32) * lk2.astype(f32))) + lam_init)
    scale = DIFF_SUB ** -0.5
    nb = length // BLOCK
    outs = []
    for i in range(nb):
        lk = (i + 1) * BLOCK
        sl = slice(i * BLOCK, (i + 1) * BLOCK)
        q_idx = i * BLOCK + jnp.arange(BLOCK)
        key_idx = jnp.arange(lk)
        m = (key_idx[None, :] <= q_idx[:, None]) & (key_idx[None, :] >= PAD_FRONT)
        s1 = jnp.einsum('bhqd,bhkd->bhqk', q1[:, :, sl], k1[:, :, :lk]).astype(f32) * scale
        s2 = jnp.einsum('bhqd,bhkd->bhqk', q2[:, :, sl], k2[:, :, :lk]).astype(f32) * scale
        a = (jax.nn.softmax(jnp.where(m, s1, NEG_BIG), axis=-1)
             - lam * jax.nn.softmax(jnp.where(m, s2, NEG_BIG), axis=-1))
        outs.append(jnp.einsum('bhqk,bhkd->bhqd', a.astype(v.dtype), v[:, :, :lk]))
    out = jnp.concatenate(outs, axis=2)
    out = rms_norm(out, norm_g) * jnp.asarray(1.0 - lam_init, out.dtype)
    return _unheads(out)


def short_gated_conv(gate_b, gate_c, h, conv_w, valid):
    length = h.shape[1]
    u = gate_c * h * valid[None, :, None].astype(h.dtype)
    u_pad = jnp.pad(u, ((0, 0), (CONV_K - 1, 0), (0, 0)))
    conv = sum(conv_w[j] * u_pad[:, j:j + length] for j in range(CONV_K))
    return gate_b * conv


def grouped_moe(h, router_w, router_bias, w_gate, w_up, w_down):
    f32 = jnp.float32
    logits = jnp.einsum('bld,de->ble', h, router_w).astype(f32)
    probs = jax.nn.softmax(logits, axis=-1)
    sel = probs + router_bias.astype(f32)
    sel_g = sel.reshape(sel.shape[:-1] + (N_GROUPS, EXPERTS_PER_GROUP))
    group_score = lax.top_k(sel_g, TOP_K)[0].sum(-1)
    g = jnp.argmax(group_score, axis=-1)
    in_g = jnp.take_along_axis(sel_g, g[..., None, None], axis=-2)[..., 0, :]
    _, local = lax.top_k(in_g, TOP_K)
    eidx = g[..., None] * EXPERTS_PER_GROUP + local
    wsel = jnp.take_along_axis(probs, eidx, axis=-1)
    wsel = wsel / jnp.sum(wsel, axis=-1, keepdims=True)
    gates = jnp.sum(jax.nn.one_hot(eidx, N_EXPERTS, dtype=f32) * wsel[..., None], axis=-2).astype(h.dtype)
    hid = (jax.nn.silu(jnp.einsum('bld,edf->blef', h, w_gate))
           * jnp.einsum('bld,edf->blef', h, w_up) * gates[..., None])
    return jnp.einsum('blef,efd->bld', hid, w_down)


def setup_inputs(seed: int = 0) -> dict:
    key = jax.random.key(seed)
    ks = jax.random.split(key, 20)
    nrm = jax.random.normal
    f32 = jnp.float32
    return {
        "x": nrm(ks[0], (BATCH, SEQ, D_MODEL), f32),
        "meta_tokens": nrm(ks[1], (N_META, D_MODEL), f32),
        "w_in": nrm(ks[2], (DEPTH, D_MODEL, IN_COLS), f32) * D_MODEL ** -0.5,
        "conv_w": nrm(ks[3], (DEPTH, CONV_K, CONV_W), f32) * CONV_K ** -0.5,
        "lambda_q1": nrm(ks[4], (DEPTH, DIFF_SUB), f32) * 0.1,
        "lambda_k1": nrm(ks[5], (DEPTH, DIFF_SUB), f32) * 0.1,
        "lambda_q2": nrm(ks[6], (DEPTH, DIFF_SUB), f32) * 0.1,
        "lambda_k2": nrm(ks[7], (DEPTH, DIFF_SUB), f32) * 0.1,
        "diff_norm_g": 1.0 + 0.02 * nrm(ks[8], (DEPTH, HEAD_DIM), f32),
        "w_out": nrm(ks[9], (DEPTH, MIX_W, D_MODEL), f32) * MIX_W ** -0.5 * BETA_INIT,
        "ln1_g": 1.0 + 0.02 * nrm(ks[10], (DEPTH, D_MODEL), f32),
        "ln1_b": 0.02 * nrm(ks[11], (DEPTH, D_MODEL), f32),
        "router_w": nrm(ks[12], (D_MODEL, N_EXPERTS), f32) * D_MODEL ** -0.5,
        "router_bias": 0.01 * nrm(ks[13], (N_EXPERTS,), f32),
        "w_gate": nrm(ks[14], (DEPTH, N_EXPERTS, D_MODEL, D_FF_EXPERT), f32) * D_MODEL ** -0.5,
        "w_up": nrm(ks[15], (DEPTH, N_EXPERTS, D_MODEL, D_FF_EXPERT), f32) * D_MODEL ** -0.5,
        "w_down": nrm(ks[16], (DEPTH, N_EXPERTS, D_FF_EXPERT, D_MODEL), f32) * D_FF_EXPERT ** -0.5 * BETA_INIT,
        "ln2_g": 1.0 + 0.02 * nrm(ks[17], (DEPTH, D_MODEL), f32),
        "ln2_b": 0.02 * nrm(ks[18], (DEPTH, D_MODEL), f32),
    }


def reference(x, meta_tokens, w_in, conv_w, lambda_q1, lambda_k1, lambda_q2, lambda_k2,
              diff_norm_g, w_out, ln1_g, ln1_b, router_w, router_bias, w_gate, w_up, w_down,
              ln2_g, ln2_b):
    b = x.shape[0]
    lead = jnp.concatenate([
        jnp.zeros((PAD_FRONT, D_MODEL), x.dtype), meta_tokens.astype(x.dtype)], axis=0)
    h = jnp.concatenate([jnp.broadcast_to(lead[None], (b, BLOCK, D_MODEL)), x], axis=1)
    length = h.shape[1]
    valid = jnp.arange(length) >= PAD_FRONT
    cos, sin = rope_tables(length, x.dtype)
    o1 = 3 * SB_W
    o2 = o1 + 3 * DIFF_W
    for l in range(DEPTH):
        lam_init = 0.8 - 0.6 * math.exp(-0.3 * l)
        proj = h @ w_in[l]
        sq, sk, sv = jnp.split(proj[..., :o1], 3, axis=-1)
        dq, dk, dv = jnp.split(proj[..., o1:o2], 3, axis=-1)
        cb, cc, ch = jnp.split(proj[..., o2:], 3, axis=-1)
        y_sb = _unheads(stick_breaking_attention(_heads(sq, SB_HEADS), _heads(sk, SB_HEADS), _heads(sv, SB_HEADS)))
        y_diff = differential_attention(dq, dk, dv, lambda_q1[l], lambda_k1[l], lambda_q2[l], lambda_k2[l],
                                        diff_norm_g[l], lam_init, cos, sin)
        y_conv = short_gated_conv(cb, cc, ch, conv_w[l], valid)
        mix = jnp.concatenate([y_sb, y_diff, y_conv], axis=-1) @ w_out[l]
        h = layer_norm(ALPHA * h + mix, ln1_g[l], ln1_b[l])
        ffn = grouped_moe(h, router_w, router_bias, w_gate[l], w_up[l], w_down[l])
        h = layer_norm(ALPHA * h + ffn, ln2_g[l], ln2_b[l])
    return h[:, BLOCK:]
```

```python
import functools
import math

import jax
import jax.numpy as jnp
from jax import lax
from jax.experimental import pallas as pl
from jax.experimental.pallas import tpu as pltpu

D_MODEL = 1024
DEPTH = 4
N_META = 16
BLOCK = 128
PAD_FRONT = BLOCK - N_META
HEAD_DIM = 64
N_HEADS = 4
ATT_W = N_HEADS * HEAD_DIM
CONV_W = 512
IN_COLS = 6 * ATT_W + 3 * CONV_W
DIFF_SUB = HEAD_DIM // 2
ROPE_DIM = DIFF_SUB // 4
ROPE_THETA = 500000.0
CONV_K = 3
N_EXPERTS = 16
N_GROUPS = 4
EXPERTS_PER_GROUP = N_EXPERTS // N_GROUPS
D_FF_EXPERT = 128
FF_W = N_EXPERTS * D_FF_EXPERT
FF_CHUNK = 512
ALPHA = (2.0 * DEPTH) ** 0.25
LN_EPS = 1e-5
NEG_BIG = -1e30

TQ = 256
KB = 256
CONV_HALO = 8

F32 = jnp.float32
BF16 = jnp.bfloat16
VMEM_LIMIT = 56 * 1024 * 1024


def _dot(a, b):
    return jnp.dot(a, b, preferred_element_type=F32)


def _dot_nt(a, b):
    return lax.dot_general(a, b, (((1,), (1,)), ((), ())), preferred_element_type=F32)


def _in_proj_kernel(h_ref, w_ref, rc_ref, rm_ref, rp_ref, cw_ref,
                    sqT_ref, sk_ref, svT_ref, dqT_ref, dk_ref, dvT_ref, yc_ref, ubuf_ref, *, tm):
    j = pl.program_id(1)
    x = h_ref[0].astype(BF16)

    def proj(c0, width):
        return _dot(x, w_ref[:, c0:c0 + width])

    def store_t(o_ref, val):
        for s in range(tm // TQ):
            o_ref[0, s] = val[s * TQ:(s + 1) * TQ, :].T.astype(o_ref.dtype)

    def rope(t):
        halves = []
        for half in range(ATT_W // 128):
            sl = slice(half * 128, (half + 1) * 128)
            th = t[:, sl]
            halves.append(th * rc_ref[:, sl]
                          + pltpu.roll(th, 128 - ROPE_DIM // 2, axis=1) * rm_ref[:, sl]
                          + pltpu.roll(th, ROPE_DIM // 2, axis=1) * rp_ref[:, sl])
        return jnp.concatenate(halves, axis=1)

    store_t(sqT_ref, proj(0, ATT_W) * (HEAD_DIM ** -0.5))
    sk_ref[0] = proj(ATT_W, ATT_W).astype(BF16)
    store_t(svT_ref, proj(2 * ATT_W, ATT_W))
    store_t(dqT_ref, rope(proj(3 * ATT_W, ATT_W)) * (DIFF_SUB ** -0.5))
    dk_ref[0] = rope(proj(4 * ATT_W, ATT_W)).astype(BF16)
    store_t(dvT_ref, proj(5 * ATT_W, ATT_W))

    c0 = 6 * ATT_W
    cb = proj(c0, CONV_W)
    cc = proj(c0 + CONV_W, CONV_W)
    ch = proj(c0 + 2 * CONV_W, CONV_W)
    pos = j * tm + lax.broadcasted_iota(jnp.int32, (tm, CONV_W), 0)
    u = jnp.where(pos >= PAD_FRONT, cc * ch, 0.0)

    @pl.when(j == 0)
    def _():
        ubuf_ref[0:CONV_HALO, :] = jnp.zeros((CONV_HALO, CONV_W), F32)

    ubuf_ref[CONV_HALO:CONV_HALO + tm, :] = u
    u1 = ubuf_ref[CONV_HALO - 1:CONV_HALO - 1 + tm, :]
    u2 = ubuf_ref[CONV_HALO - 2:CONV_HALO - 2 + tm, :]
    conv = cw_ref[0:1, :] * u2 + cw_ref[1:2, :] * u1 + cw_ref[2:3, :] * u
    yc_ref[0] = (cb * conv).astype(BF16)
    ubuf_ref[0:CONV_HALO, :] = ubuf_ref[tm:tm + CONV_HALO, :]


def _in_proj(h, w, rc, rm, rp, cw):
    b, lp, _ = h.shape
    tm = 768 if lp % 768 == 0 else TQ
    nj = lp // tm
    nq = lp // TQ
    t_shape = jax.ShapeDtypeStruct((b, nq, ATT_W, TQ), BF16)
    n_shape = jax.ShapeDtypeStruct((b, lp, ATT_W), BF16)
    t_spec = pl.BlockSpec((1, tm // TQ, ATT_W, TQ), lambda bi, j: (bi, j, 0, 0))
    n_spec = pl.BlockSpec((1, tm, ATT_W), lambda bi, j: (bi, j, 0))
    tab_spec = pl.BlockSpec((tm, ATT_W), lambda bi, j: (j, 0))
    return pl.pallas_call(
        functools.partial(_in_proj_kernel, tm=tm),
        grid=(b, nj),
        in_specs=[
            pl.BlockSpec((1, tm, D_MODEL), lambda bi, j: (bi, j, 0)),
            pl.BlockSpec((D_MODEL, IN_COLS), lambda bi, j: (0, 0)),
            tab_spec, tab_spec, tab_spec,
            pl.BlockSpec((CONV_K, CONV_W), lambda bi, j: (0, 0)),
        ],
        out_specs=[t_spec, n_spec, t_spec, t_spec, n_spec, t_spec,
                   pl.BlockSpec((1, tm, CONV_W), lambda bi, j: (bi, j, 0))],
        out_shape=[t_shape, n_shape, t_shape, t_shape, n_shape, t_shape,
                   jax.ShapeDtypeStruct((b, lp, CONV_W), BF16)],
        scratch_shapes=[pltpu.VMEM((tm + CONV_HALO, CONV_W), F32)],
        compiler_params=pltpu.CompilerParams(
            dimension_semantics=("arbitrary", "arbitrary"), vmem_limit_bytes=VMEM_LIMIT),
        name="in_proj",
    )(h, w, rc, rm, rp, cw)


def _masked_rows(q_t, width):
    frow = lax.broadcasted_iota(jnp.int32, q_t.shape, 0)
    zero = jnp.zeros_like(q_t)
    return [jnp.where((frow >= i * width) & (frow < (i + 1) * width), q_t, zero)
            for i in range(ATT_W // width)]


def _causal_sweep(qi, block):
    block(qi, True)

    def body(i, carry):
        block(qi - 1 - i, False)
        return carry

    lax.fori_loop(0, jnp.maximum(qi - 1, 0), body, 0)

    @pl.when(qi > 0)
    def _():
        block(0, True)


def _att_specs(b, lp):
    nq = lp // TQ
    nkb = lp // KB
    in_specs = [
        pl.BlockSpec((1, 1, ATT_W, TQ), lambda bi, qi: (bi, qi, 0, 0)),
        pl.BlockSpec((1, lp, ATT_W), lambda bi, qi: (bi, 0, 0)),
        pl.BlockSpec((1, nkb, ATT_W, KB), lambda bi, qi: (bi, 0, 0, 0)),
    ]
    out_spec = pl.BlockSpec((1, TQ, ATT_W), lambda bi, qi: (bi, qi, 0))
    out_shape = jax.ShapeDtypeStruct((b, lp, ATT_W), BF16)
    return (b, nq), in_specs, out_spec, out_shape


def _sb_kernel(qT_ref, k_ref, vT_ref, o_ref, acc_ref, later_ref):
    qi = pl.program_id(1)
    qm = _masked_rows(qT_ref[0, 0], HEAD_DIM)
    si = lax.broadcasted_iota(jnp.int32, (KB, KB), 0)
    ji = lax.broadcasted_iota(jnp.int32, (KB, KB), 1)
    upper = (ji > si).astype(BF16)
    acc_ref[...] = jnp.zeros_like(acc_ref)
    later_ref[...] = jnp.zeros_like(later_ref)
    q_idx = qi * TQ + lax.broadcasted_iota(jnp.int32, (KB, TQ), 1)
    krow = lax.broadcasted_iota(jnp.int32, (KB, TQ), 0)

    def block(kb, masked):
        kblk = k_ref[0, pl.ds(pl.multiple_of(kb * KB, KB), KB), :]
        vblk = vT_ref[0, kb]
        if masked:
            key_idx = kb * KB + krow
            m = (key_idx < q_idx) & (key_idx >= PAD_FRONT)
        for h in range(N_HEADS):
            hs = slice(h * HEAD_DIM, (h + 1) * HEAD_DIM)
            z = _dot(kblk, qm[h])
            sp = jnp.log(1.0 + jnp.exp(-jnp.abs(z)))
            log_rem = -jnp.maximum(z, 0.0) - sp
            if masked:
                log_rem = jnp.where(m, log_rem, 0.0)
            intra = _dot(upper, log_rem.astype(BF16))
            later = later_ref[h:h + 1, :]
            w = jnp.exp(jnp.minimum(z, 0.0) - sp + intra + later)
            if masked:
                w = jnp.where(m, w, 0.0)
            acc_ref[hs, :] += _dot(vblk[hs, :], w.astype(BF16))
            later_ref[h:h + 1, :] = later + jnp.sum(log_rem, axis=0, keepdims=True)

    _causal_sweep(qi, block)
    o_ref[0] = acc_ref[...].T.astype(o_ref.dtype)


def _sb_attention(q_t, k, v_t):
    b, lp, _ = k.shape
    grid, in_specs, out_spec, out_shape = _att_specs(b, lp)
    return pl.pallas_call(
        _sb_kernel,
        grid=grid,
        in_specs=in_specs,
        out_specs=out_spec,
        out_shape=out_shape,
        scratch_shapes=[pltpu.VMEM((ATT_W, TQ), F32), pltpu.VMEM((8, TQ), F32)],
        compiler_params=pltpu.CompilerParams(
            dimension_semantics=("arbitrary", "arbitrary"), vmem_limit_bytes=VMEM_LIMIT),
        name="sb_attention",
    )(q_t, k, v_t)


def _diff_kernel(qT_ref, k_ref, vT_ref, lamp_ref, li_ref, g_ref, o_ref, acc_ref, m_ref, l_ref):
    qi = pl.program_id(1)
    qm = _masked_rows(qT_ref[0, 0], DIFF_SUB)
    acc_ref[...] = jnp.zeros_like(acc_ref)
    l_ref[...] = jnp.zeros_like(l_ref)
    m_ref[...] = jnp.full_like(m_ref, NEG_BIG)
    q_idx = qi * TQ + lax.broadcasted_iota(jnp.int32, (KB, TQ), 1)
    krow = lax.broadcasted_iota(jnp.int32, (KB, TQ), 0)

    def block(kb, masked):
        kblk = k_ref[0, pl.ds(pl.multiple_of(kb * KB, KB), KB), :]
        vblk = vT_ref[0, kb]
        if masked:
            key_idx = kb * KB + krow
            m = (key_idx <= q_idx) & (key_idx >= PAD_FRONT)
        for h in range(N_HEADS):
            vh = vblk[h * HEAD_DIM:(h + 1) * HEAD_DIM, :]
            for c in range(2):
                i = 2 * h + c
                rows = slice(i * HEAD_DIM, (i + 1) * HEAD_DIM)
                s = _dot(kblk, qm[i])
                if masked:
                    s = jnp.where(m, s, NEG_BIG)
                m_old = m_ref[i:i + 1, :]
                m_new = jnp.maximum(m_old, jnp.max(s, axis=0, keepdims=True))
                alpha = jnp.exp(m_old - m_new)
                p = jnp.exp(s - m_new)
                l_ref[i:i + 1, :] = alpha * l_ref[i:i + 1, :] + jnp.sum(p, axis=0, keepdims=True)
                acc_ref[rows, :] = alpha * acc_ref[rows, :] + _dot(vh, p.astype(BF16))
                m_ref[i:i + 1, :] = m_new

    _causal_sweep(qi, block)

    lp = lamp_ref[...]
    lam_init = li_ref[...]
    lam = (jnp.exp(jnp.sum(lp[0:1] * lp[1:2], axis=1, keepdims=True))
           - jnp.exp(jnp.sum(lp[2:3] * lp[3:4], axis=1, keepdims=True)) + lam_init)
    outs = []
    for h in range(N_HEADS):
        r1 = slice(2 * h * HEAD_DIM, (2 * h + 1) * HEAD_DIM)
        r2 = slice((2 * h + 1) * HEAD_DIM, (2 * h + 2) * HEAD_DIM)
        o = acc_ref[r1, :] / l_ref[2 * h:2 * h + 1, :] - lam * (acc_ref[r2, :] / l_ref[2 * h + 1:2 * h + 2, :])
        ms = jnp.mean(o * o, axis=0, keepdims=True)
        outs.append(o * lax.rsqrt(ms + LN_EPS) * g_ref[...] * (1.0 - lam_init))
    o_ref[0] = jnp.concatenate(outs, axis=0).T.astype(o_ref.dtype)


def _diff_attention(q_t, k, v_t, lamp, lam_init, g):
    b, lp, _ = k.shape
    grid, in_specs, out_spec, out_shape = _att_specs(b, lp)
    in_specs = in_specs + [
        pl.BlockSpec((4, DIFF_SUB), lambda bi, qi: (0, 0)),
        pl.BlockSpec((1, 1), lambda bi, qi: (0, 0)),
        pl.BlockSpec((HEAD_DIM, 1), lambda bi, qi: (0, 0)),
    ]
    return pl.pallas_call(
        _diff_kernel,
        grid=grid,
        in_specs=in_specs,
        out_specs=out_spec,
        out_shape=out_shape,
        scratch_shapes=[pltpu.VMEM((2 * ATT_W, TQ), F32), pltpu.VMEM((8, TQ), F32),
                        pltpu.VMEM((8, TQ), F32)],
        compiler_params=pltpu.CompilerParams(
            dimension_semantics=("arbitrary", "arbitrary"), vmem_limit_bytes=VMEM_LIMIT),
        name="diff_attention",
    )(q_t, k, v_t, lamp, lam_init, g)


def _layer_norm(x, g, b):
    mu = jnp.mean(x, axis=-1, keepdims=True)
    xc = x - mu
    var = jnp.mean(xc * xc, axis=-1, keepdims=True)
    return xc * lax.rsqrt(var + LN_EPS) * g + b


def _route(logits_t, bias_ref):
    lg = [logits_t[e:e + 1, :] for e in range(N_EXPERTS)]
    mx = functools.reduce(jnp.maximum, lg)
    ex = [jnp.exp(v - mx) for v in lg]
    den = functools.reduce(jnp.add, ex)
    probs = [v / den for v in ex]
    sel = [probs[e] + bias_ref[e:e + 1, :] for e in range(N_EXPERTS)]

    group_score = []
    for g in range(N_GROUPS):
        a = sel[g * EXPERTS_PER_GROUP:(g + 1) * EXPERTS_PER_GROUP]
        pair_sums = [a[i] + a[j] for i in range(EXPERTS_PER_GROUP) for j in range(i + 1, EXPERTS_PER_GROUP)]
        group_score.append(functools.reduce(jnp.maximum, pair_sums))
    best = group_score[0]
    gidx = jnp.zeros(best.shape, jnp.int32)
    for g in range(1, N_GROUPS):
        better = group_score[g] > best
        gidx = jnp.where(better, g, gidx)
        best = jnp.where(better, group_score[g], best)

    def in_group(vals):
        out = []
        for j in range(EXPERTS_PER_GROUP):
            v = vals[(N_GROUPS - 1) * EXPERTS_PER_GROUP + j]
            for g in range(N_GROUPS - 2, -1, -1):
                v = jnp.where(gidx == g, vals[g * EXPERTS_PER_GROUP + j], v)
            out.append(v)
        return out

    s_in = in_group(sel)
    p_in = in_group(probs)
    i1 = jnp.zeros(best.shape, jnp.int32)
    v1 = s_in[0]
    for j in range(1, EXPERTS_PER_GROUP):
        better = s_in[j] > v1
        i1 = jnp.where(better, j, i1)
        v1 = jnp.where(better, s_in[j], v1)
    i2 = jnp.full(best.shape, -1, jnp.int32)
    v2 = jnp.full(best.shape, -jnp.inf, F32)
    for j in range(EXPERTS_PER_GROUP):
        better = (i1 != j) & (s_in[j] > v2)
        i2 = jnp.where(better, j, i2)
        v2 = jnp.where(better, s_in[j], v2)
    w1 = functools.reduce(jnp.add, [jnp.where(i1 == j, p_in[j], 0.0) for j in range(EXPERTS_PER_GROUP)])
    w2 = functools.reduce(jnp.add, [jnp.where(i2 == j, p_in[j], 0.0) for j in range(EXPERTS_PER_GROUP)])
    wsum = w1 + w2
    w1 = w1 / wsum
    w2 = w2 / wsum
    gates = []
    for e in range(N_EXPERTS):
        g, j = divmod(e, EXPERTS_PER_GROUP)
        in_g = gidx == g
        gates.append(jnp.where(in_g & (i1 == j), w1, 0.0) + jnp.where(in_g & (i2 == j), w2, 0.0))
    return gates


def _post_kernel(ysb_ref, ydf_ref, ycv_ref, h_ref, wo_ref, g1_ref, b1_ref, rw_ref, rb_ref,
                 wg_ref, wu_ref, wd_ref, g2_ref, b2_ref, o_ref, gate_ref):
    @pl.when(pl.program_id(0) == 0)
    def _():
        gate_ref[...] = jnp.zeros_like(gate_ref)

    mix = (_dot(ysb_ref[...], wo_ref[0:ATT_W, :])
           + _dot(ydf_ref[...], wo_ref[ATT_W:2 * ATT_W, :])
           + _dot(ycv_ref[...], wo_ref[2 * ATT_W:, :]))
    h1 = _layer_norm(ALPHA * h_ref[...] + mix, g1_ref[...], b1_ref[...])

    x_hi = h1.astype(BF16)
    x_lo = (h1 - x_hi.astype(F32)).astype(BF16)
    logits_t = _dot_nt(rw_ref[0], x_hi) + _dot_nt(rw_ref[1], x_hi) + _dot_nt(rw_ref[0], x_lo)
    gates = _route(logits_t, rb_ref)
    for e in range(N_EXPERTS):
        gate_ref[e:e + 1, :] = gates[e]
    gate_cols = gate_ref[...].T

    ffn = jnp.zeros(h1.shape, F32)
    for c in range(FF_W // FF_CHUNK):
        cs = slice(c * FF_CHUNK, (c + 1) * FF_CHUNK)
        gate_act = _dot(x_hi, wg_ref[:, cs])
        up = _dot(x_hi, wu_ref[:, cs])
        hid = gate_act * (1.0 / (1.0 + jnp.exp(-gate_act))) * up
        parts = []
        for k in range(FF_CHUNK // D_FF_EXPERT):
            e = c * (FF_CHUNK // D_FF_EXPERT) + k
            parts.append(hid[:, k * D_FF_EXPERT:(k + 1) * D_FF_EXPERT] * gate_cols[:, e:e + 1])
        ffn = ffn + _dot(jnp.concatenate(parts, axis=1).astype(BF16), wd_ref[cs, :])
    o_ref[...] = _layer_norm(ALPHA * h1 + ffn, g2_ref[...], b2_ref[...])


def _post(ysb, ydf, ycv, h, wo, g1, b1, rw, rb, wg, wu, wd, g2, b2):
    t = h.shape[0]
    tm = 512 if t % 512 == 0 else TQ
    row = lambda w: pl.BlockSpec((tm, w), lambda i: (i, 0))
    full = lambda shape: pl.BlockSpec(shape, lambda i: (0,) * len(shape))
    return pl.pallas_call(
        _post_kernel,
        grid=(t // tm,),
        in_specs=[row(ATT_W), row(ATT_W), row(CONV_W), row(D_MODEL),
                  full((D_MODEL, D_MODEL)), full((1, D_MODEL)), full((1, D_MODEL)),
                  full((2, N_EXPERTS, D_MODEL)), full((N_EXPERTS, 1)),
                  full((D_MODEL, FF_W)), full((D_MODEL, FF_W)), full((FF_W, D_MODEL)),
                  full((1, D_MODEL)), full((1, D_MODEL))],
        out_specs=row(D_MODEL),
        out_shape=jax.ShapeDtypeStruct((t, D_MODEL), F32),
        scratch_shapes=[pltpu.VMEM((128, tm), F32)],
        compiler_params=pltpu.CompilerParams(
            dimension_semantics=("arbitrary",), vmem_limit_bytes=VMEM_LIMIT),
        name="out_proj_moe",
    )(ysb, ydf, ycv, h, wo, g1, b1, rw, rb, wg, wu, wd, g2, b2)


def _rope_tables(lp):
    half = ROPE_DIM // 2
    pos = (jnp.arange(lp) - PAD_FRONT).astype(F32)
    inv = jnp.power(jnp.float32(ROPE_THETA), -jnp.arange(0, ROPE_DIM, 2, dtype=F32) / ROPE_DIM)
    ang = pos[:, None] * inv[None, :]
    cos, sin = jnp.cos(ang), jnp.sin(ang)
    pad = jnp.zeros((lp, DIFF_SUB - ROPE_DIM), F32)
    zero = jnp.zeros((lp, half), F32)
    rc = jnp.concatenate([cos, cos, pad + 1.0], axis=1)
    rm = jnp.concatenate([-sin, zero, pad], axis=1)
    rp = jnp.concatenate([zero, sin, pad], axis=1)
    reps = ATT_W // DIFF_SUB
    return tuple(jnp.tile(t, (1, reps)) for t in (rc, rm, rp))


def kernel(x, meta_tokens, w_in, conv_w, lambda_q1, lambda_k1, lambda_q2, lambda_k2, diff_norm_g, w_out,
           ln1_g, ln1_b, router_w, router_bias, w_gate, w_up, w_down, ln2_g, ln2_b):
    b, seq, _ = x.shape
    length = BLOCK + seq
    lp = -(-length // TQ) * TQ
    lead = jnp.concatenate([jnp.zeros((PAD_FRONT, D_MODEL), x.dtype), meta_tokens.astype(x.dtype)], axis=0)
    h = jnp.concatenate([jnp.broadcast_to(lead[None], (b, BLOCK, D_MODEL)), x,
                         jnp.zeros((b, lp - length, D_MODEL), x.dtype)], axis=1)
    rc, rm, rp = _rope_tables(lp)

    rw_t = router_w.T.astype(F32)
    rw_hi = rw_t.astype(BF16)
    rw_lo = (rw_t - rw_hi.astype(F32)).astype(BF16)
    rw = jnp.stack([rw_hi, rw_lo])
    rb = router_bias.astype(F32).reshape(N_EXPERTS, 1)

    for l in range(DEPTH):
        lam_init = jnp.full((1, 1), 0.8 - 0.6 * math.exp(-0.3 * l), F32)
        lamp = jnp.stack([lambda_q1[l], lambda_k1[l], lambda_q2[l], lambda_k2[l]]).astype(F32)
        sq_t, sk, sv_t, dq_t, dk, dv_t, y_conv = _in_proj(
            h, w_in[l].astype(BF16), rc, rm, rp, conv_w[l].astype(F32))
        y_sb = _sb_attention(sq_t, sk, sv_t)
        y_diff = _diff_attention(dq_t, dk, dv_t, lamp, lam_init, diff_norm_g[l].astype(F32).reshape(HEAD_DIM, 1))
        wg = w_gate[l].transpose(1, 0, 2).reshape(D_MODEL, FF_W).astype(BF16)
        wu = w_up[l].transpose(1, 0, 2).reshape(D_MODEL, FF_W).astype(BF16)
        wd = w_down[l].reshape(FF_W, D_MODEL).astype(BF16)
        flat = lambda a: a.reshape(b * lp, a.shape[-1])
        h = _post(flat(y_sb), flat(y_diff), flat(y_conv), flat(h), w_out[l].astype(BF16),
                  ln1_g[l].reshape(1, D_MODEL), ln1_b[l].reshape(1, D_MODEL), rw, rb, wg, wu, wd,
                  ln2_g[l].reshape(1, D_MODEL), ln2_b[l].reshape(1, D_MODEL)).reshape(b, lp, D_MODEL)
    return h[:, BLOCK:length]
```

```python
import functools
import math

import jax
import jax.numpy as jnp
from jax import lax
from jax.experimental import pallas as pl
from jax.experimental.pallas import tpu as pltpu

D_MODEL = 1024
DEPTH = 4
N_META = 16
BLOCK = 128
PAD_FRONT = BLOCK - N_META
HEAD_DIM = 64
N_HEADS = 4
ATT_W = N_HEADS * HEAD_DIM
CONV_W = 512
IN_COLS = 6 * ATT_W + 3 * CONV_W
DIFF_SUB = HEAD_DIM // 2
ROPE_DIM = DIFF_SUB // 4
ROPE_THETA = 500000.0
CONV_K = 3
N_EXPERTS = 16
N_GROUPS = 4
EXPERTS_PER_GROUP = N_EXPERTS // N_GROUPS
D_FF_EXPERT = 128
FF_W = N_EXPERTS * D_FF_EXPERT
FF_CHUNK = 512
ALPHA = (2.0 * DEPTH) ** 0.25
LN_EPS = 1e-5
NEG_BIG = -1e30

TQ = 256
KB = 256
CONV_HALO = 8

SUM_ROWS = 16
V_ROWS = HEAD_DIM + SUM_ROWS
LOG2E = math.log2(math.e)
SIGN_BIT = -2 ** 31

F32 = jnp.float32
BF16 = jnp.bfloat16
VMEM_LIMIT = 56 * 1024 * 1024


def _dot(a, b):
    return jnp.dot(a, b, preferred_element_type=F32)


def _dot_nt(a, b):
    return lax.dot_general(a, b, (((1,), (1,)), ((), ())), preferred_element_type=F32)


def _in_proj_kernel(h_ref, w_ref, rc_ref, rm_ref, rp_ref, cw_ref,
                    sqT_ref, sk_ref, svT_ref, dqT_ref, dk_ref, dvT_ref, yc_ref, ubuf_ref, *, tm):
    j = pl.program_id(1)
    x = h_ref[0].astype(BF16)

    def proj(c0, width):
        return _dot(x, w_ref[:, c0:c0 + width])

    def store_t(o_ref, val):
        for s in range(tm // TQ):
            o_ref[0, s] = val[s * TQ:(s + 1) * TQ, :].T.astype(o_ref.dtype)

    def rope(t):
        halves = []
        for half in range(ATT_W // 128):
            sl = slice(half * 128, (half + 1) * 128)
            th = t[:, sl]
            halves.append(th * rc_ref[:, sl]
                          + pltpu.roll(th, 128 - ROPE_DIM // 2, axis=1) * rm_ref[:, sl]
                          + pltpu.roll(th, ROPE_DIM // 2, axis=1) * rp_ref[:, sl])
        return jnp.concatenate(halves, axis=1)

    store_t(sqT_ref, proj(0, ATT_W) * (HEAD_DIM ** -0.5 * LOG2E))
    sk_ref[0] = proj(ATT_W, ATT_W).astype(BF16)
    store_t(svT_ref, proj(2 * ATT_W, ATT_W))
    store_t(dqT_ref, rope(proj(3 * ATT_W, ATT_W)) * (DIFF_SUB ** -0.5 * LOG2E))
    dk_ref[0] = rope(proj(4 * ATT_W, ATT_W)).astype(BF16)
    dv = proj(5 * ATT_W, ATT_W)
    ones = jnp.ones((V_ROWS - HEAD_DIM, TQ), BF16)
    for s in range(tm // TQ):
        dv_t = dv[s * TQ:(s + 1) * TQ, :].T.astype(BF16)
        for h in range(N_HEADS):
            dvT_ref[0, s, h * V_ROWS:h * V_ROWS + HEAD_DIM, :] = dv_t[h * HEAD_DIM:(h + 1) * HEAD_DIM, :]
            dvT_ref[0, s, h * V_ROWS + HEAD_DIM:(h + 1) * V_ROWS, :] = ones

    c0 = 6 * ATT_W
    cb = proj(c0, CONV_W)
    cc = proj(c0 + CONV_W, CONV_W)
    ch = proj(c0 + 2 * CONV_W, CONV_W)
    pos = j * tm + lax.broadcasted_iota(jnp.int32, (tm, CONV_W), 0)
    u = jnp.where(pos >= PAD_FRONT, cc * ch, 0.0)

    @pl.when(j == 0)
    def _():
        ubuf_ref[0:CONV_HALO, :] = jnp.zeros((CONV_HALO, CONV_W), F32)

    ubuf_ref[CONV_HALO:CONV_HALO + tm, :] = u
    u1 = ubuf_ref[CONV_HALO - 1:CONV_HALO - 1 + tm, :]
    u2 = ubuf_ref[CONV_HALO - 2:CONV_HALO - 2 + tm, :]
    conv = cw_ref[0:1, :] * u2 + cw_ref[1:2, :] * u1 + cw_ref[2:3, :] * u
    yc_ref[0] = (cb * conv).astype(BF16)
    ubuf_ref[0:CONV_HALO, :] = ubuf_ref[tm:tm + CONV_HALO, :]


def _in_proj(h, w, rc, rm, rp, cw):
    b, lp, _ = h.shape
    tm = 768 if lp % 768 == 0 else TQ
    nj = lp // tm
    nq = lp // TQ
    t_shape = jax.ShapeDtypeStruct((b, nq, ATT_W, TQ), BF16)
    n_shape = jax.ShapeDtypeStruct((b, lp, ATT_W), BF16)
    v_shape = jax.ShapeDtypeStruct((b, nq, N_HEADS * V_ROWS, TQ), BF16)
    t_spec = pl.BlockSpec((1, tm // TQ, ATT_W, TQ), lambda bi, j: (bi, j, 0, 0))
    v_spec = pl.BlockSpec((1, tm // TQ, N_HEADS * V_ROWS, TQ), lambda bi, j: (bi, j, 0, 0))
    n_spec = pl.BlockSpec((1, tm, ATT_W), lambda bi, j: (bi, j, 0))
    tab_spec = pl.BlockSpec((tm, ATT_W), lambda bi, j: (j, 0))
    return pl.pallas_call(
        functools.partial(_in_proj_kernel, tm=tm),
        grid=(b, nj),
        in_specs=[
            pl.BlockSpec((1, tm, D_MODEL), lambda bi, j: (bi, j, 0)),
            pl.BlockSpec((D_MODEL, IN_COLS), lambda bi, j: (0, 0)),
            tab_spec, tab_spec, tab_spec,
            pl.BlockSpec((CONV_K, CONV_W), lambda bi, j: (0, 0)),
        ],
        out_specs=[t_spec, n_spec, t_spec, t_spec, n_spec, v_spec,
                   pl.BlockSpec((1, tm, CONV_W), lambda bi, j: (bi, j, 0))],
        out_shape=[t_shape, n_shape, t_shape, t_shape, n_shape, v_shape,
                   jax.ShapeDtypeStruct((b, lp, CONV_W), BF16)],
        scratch_shapes=[pltpu.VMEM((tm + CONV_HALO, CONV_W), F32)],
        compiler_params=pltpu.CompilerParams(
            dimension_semantics=("arbitrary", "arbitrary"), vmem_limit_bytes=VMEM_LIMIT),
        name="in_proj",
    )(h, w, rc, rm, rp, cw)


def _store_masked_rows(qm_ref, q_t, width):
    frow = lax.broadcasted_iota(jnp.int32, q_t.shape, 0)
    zero = jnp.zeros_like(q_t)
    for i in range(ATT_W // width):
        qm_ref[i] = jnp.where((frow >= i * width) & (frow < (i + 1) * width), q_t, zero)


def _causal_sweep(qi, stage_a, stage_b, stage_ba):
    odd = qi & 1
    stage_a(qi, odd, "diag")

    @pl.when((odd == 0) & (qi >= 2))
    def _():
        stage_ba(qi, 0, None)

    kb_odd = qi - 1 + odd

    def body(j, carry):
        kb = kb_odd - 2 * j
        stage_ba(kb, 1, None)
        stage_ba(kb - 1, 0, None)
        return carry

    lax.fori_loop(0, jnp.maximum(kb_odd - 1, 0) // 2, body, 0)

    @pl.when(qi >= 1)
    def _():
        stage_ba(1, 1, "first")

    stage_b(0, 0)


def _key_mask(kind, qi, kb, inclusive):
    key_idx = kb * KB + lax.broadcasted_iota(jnp.int32, (KB, TQ), 0)
    valid = key_idx >= PAD_FRONT
    if kind == "diag":
        q_idx = qi * TQ + lax.broadcasted_iota(jnp.int32, (KB, TQ), 1)
        valid = valid & ((key_idx <= q_idx) if inclusive else (key_idx < q_idx))
    return valid


def _att_specs(b, lp, v_rows):
    nq = lp // TQ
    nkb = lp // KB
    in_specs = [
        pl.BlockSpec((1, 1, ATT_W, TQ), lambda bi, qi: (bi, qi, 0, 0)),
        pl.BlockSpec((1, lp, ATT_W), lambda bi, qi: (bi, 0, 0)),
        pl.BlockSpec((1, nkb, v_rows, KB), lambda bi, qi: (bi, 0, 0, 0)),
    ]
    out_spec = pl.BlockSpec((1, TQ, ATT_W), lambda bi, qi: (bi, qi, 0))
    out_shape = jax.ShapeDtypeStruct((b, lp, ATT_W), BF16)
    return (b, nq), in_specs, out_spec, out_shape


def _sb_kernel(qT_ref, k_ref, vT_ref, o_ref, qm_ref, upper_ref, lb_ref, lr_ref, acc_ref, later_ref):
    qi = pl.program_id(1)
    _store_masked_rows(qm_ref, qT_ref[0, 0], HEAD_DIM)
    si = lax.broadcasted_iota(jnp.int32, (KB, KB), 0)
    ji = lax.broadcasted_iota(jnp.int32, (KB, KB), 1)
    upper_ref[0:KB, :] = (ji > si).astype(BF16)
    upper_ref[KB:, :] = jnp.ones((SUM_ROWS, KB), BF16)
    acc_ref[...] = jnp.zeros_like(acc_ref)
    later_ref[...] = jnp.zeros_like(later_ref)

    def a_head(h, kb, slot, valid):
        kblk = k_ref[0, pl.ds(pl.multiple_of(kb * KB, KB), KB), :]
        z = _dot(kblk, qm_ref[h])
        if valid is not None:
            z = jnp.where(valid, z, NEG_BIG)
        neg_abs = lax.bitcast_convert_type(lax.bitcast_convert_type(z, jnp.int32) | SIGN_BIT, F32)
        sp = jnp.log(1.0 + jnp.exp2(neg_abs)) * LOG2E
        log_beta = jnp.minimum(z, 0.0) - sp
        lb_ref[slot * N_HEADS + h] = log_beta
        lr_ref[slot * N_HEADS + h] = (log_beta - z).astype(BF16)

    def b_intra(h, slot):
        return _dot(upper_ref[...], lr_ref[slot * N_HEADS + h])

    def b_value(h, kb, slot, sums):
        hs = slice(h * HEAD_DIM, (h + 1) * HEAD_DIM)
        w = jnp.exp2(lb_ref[slot * N_HEADS + h] + sums[0:KB, :])
        later = later_ref[h:h + 1, :]
        acc_ref[hs, :] += _dot(vT_ref[0, kb, hs, :], w.astype(BF16)) * jnp.exp2(later)
        later_ref[h:h + 1, :] = later + sums[KB:KB + 1, :]

    def stage_a(kb, slot, kind):
        valid = _key_mask(kind, qi, kb, inclusive=False)
        for h in range(N_HEADS):
            a_head(h, kb, slot, valid)

    def stage_b(kb, slot):
        intra = [b_intra(h, slot) for h in range(N_HEADS)]
        for h in range(N_HEADS):
            b_value(h, kb, slot, intra[h])

    def stage_ba(kb, slot, kind):
        valid = None if kind is None else _key_mask(kind, qi, kb - 1, inclusive=False)
        intra = {}
        for h in range(N_HEADS + 1):
            if h < N_HEADS:
                intra[h] = b_intra(h, slot)
                a_head(h, kb - 1, 1 - slot, valid)
            if h >= 1:
                b_value(h - 1, kb, slot, intra.pop(h - 1))

    _causal_sweep(qi, stage_a, stage_b, stage_ba)
    o_ref[0] = acc_ref[...].T.astype(o_ref.dtype)


def _sb_attention(q_t, k, v_t):
    b, lp, _ = k.shape
    grid, in_specs, out_spec, out_shape = _att_specs(b, lp, ATT_W)
    return pl.pallas_call(
        _sb_kernel,
        grid=grid,
        in_specs=in_specs,
        out_specs=out_spec,
        out_shape=out_shape,
        scratch_shapes=[pltpu.VMEM((N_HEADS, ATT_W, TQ), BF16), pltpu.VMEM((KB + SUM_ROWS, KB), BF16),
                        pltpu.VMEM((2 * N_HEADS, KB, TQ), F32), pltpu.VMEM((2 * N_HEADS, KB, TQ), BF16),
                        pltpu.VMEM((ATT_W, TQ), F32), pltpu.VMEM((8, TQ), F32)],
        compiler_params=pltpu.CompilerParams(
            dimension_semantics=("arbitrary", "arbitrary"), vmem_limit_bytes=VMEM_LIMIT),
        name="sb_attention",
    )(q_t, k, v_t)


def _diff_kernel(qT_ref, k_ref, vT_ref, lamp_ref, li_ref, g_ref, o_ref,
                 qm_ref, s_ref, bmax_ref, acc_ref, m_ref):
    qi = pl.program_id(1)
    _store_masked_rows(qm_ref, qT_ref[0, 0], DIFF_SUB)
    acc_ref[...] = jnp.zeros_like(acc_ref)
    m_ref[...] = jnp.full_like(m_ref, NEG_BIG)
    n_maps = 2 * N_HEADS

    def a_map(i, kb, slot, valid):
        kblk = k_ref[0, pl.ds(pl.multiple_of(kb * KB, KB), KB), :]
        s = _dot(kblk, qm_ref[i])
        if valid is not None:
            s = jnp.where(valid, s, NEG_BIG)
        s_ref[slot * n_maps + i] = s
        bmax_ref[slot, i:i + 1, :] = jnp.max(s, axis=0, keepdims=True)

    def b_map(i, kb, slot):
        vh = vT_ref[0, kb, (i // 2) * V_ROWS:(i // 2 + 1) * V_ROWS, :]
        m_old = m_ref[i:i + 1, :]
        m_new = jnp.maximum(m_old, bmax_ref[slot, i:i + 1, :])
        alpha = jnp.exp2(m_old - m_new)
        p = jnp.exp2(s_ref[slot * n_maps + i] - m_new)
        acc_ref[i] = alpha * acc_ref[i] + _dot(vh, p.astype(BF16))
        m_ref[i:i + 1, :] = m_new

    def stage_a(kb, slot, kind):
        valid = _key_mask(kind, qi, kb, inclusive=True)
        for i in range(n_maps):
            a_map(i, kb, slot, valid)

    def stage_b(kb, slot):
        for i in range(n_maps):
            b_map(i, kb, slot)

    def stage_ba(kb, slot, kind):
        valid = None if kind is None else _key_mask(kind, qi, kb - 1, inclusive=True)
        for i in range(n_maps):
            a_map(i, kb - 1, 1 - slot, valid)
            b_map(i, kb, slot)

    _causal_sweep(qi, stage_a, stage_b, stage_ba)

    lp = lamp_ref[...]
    lam_init = li_ref[...]
    lam = (jnp.exp(jnp.sum(lp[0:1] * lp[1:2], axis=1, keepdims=True))
           - jnp.exp(jnp.sum(lp[2:3] * lp[3:4], axis=1, keepdims=True)) + lam_init)
    outs = []
    for h in range(N_HEADS):
        a1 = acc_ref[2 * h]
        a2 = acc_ref[2 * h + 1]
        o = (a1[:HEAD_DIM] / a1[HEAD_DIM:HEAD_DIM + 1]
             - lam * (a2[:HEAD_DIM] / a2[HEAD_DIM:HEAD_DIM + 1]))
        ms = jnp.mean(o * o, axis=0, keepdims=True)
        outs.append(o * lax.rsqrt(ms + LN_EPS) * g_ref[...] * (1.0 - lam_init))
    o_ref[0] = jnp.concatenate(outs, axis=0).T.astype(o_ref.dtype)


def _diff_attention(q_t, k, v_t, lamp, lam_init, g):
    b, lp, _ = k.shape
    grid, in_specs, out_spec, out_shape = _att_specs(b, lp, N_HEADS * V_ROWS)
    in_specs = in_specs + [
        pl.BlockSpec((4, DIFF_SUB), lambda bi, qi: (0, 0)),
        pl.BlockSpec((1, 1), lambda bi, qi: (0, 0)),
        pl.BlockSpec((HEAD_DIM, 1), lambda bi, qi: (0, 0)),
    ]
    return pl.pallas_call(
        _diff_kernel,
        grid=grid,
        in_specs=in_specs,
        out_specs=out_spec,
        out_shape=out_shape,
        scratch_shapes=[pltpu.VMEM((2 * N_HEADS, ATT_W, TQ), BF16), pltpu.VMEM((4 * N_HEADS, KB, TQ), F32),
                        pltpu.VMEM((2, 8, TQ), F32),
                        pltpu.VMEM((2 * N_HEADS, V_ROWS, TQ), F32), pltpu.VMEM((8, TQ), F32)],
        compiler_params=pltpu.CompilerParams(
            dimension_semantics=("arbitrary", "arbitrary"), vmem_limit_bytes=VMEM_LIMIT),
        name="diff_attention",
    )(q_t, k, v_t, lamp, lam_init, g)


def _layer_norm(x, g, b):
    mu = jnp.mean(x, axis=-1, keepdims=True)
    xc = x - mu
    var = jnp.mean(xc * xc, axis=-1, keepdims=True)
    return xc * lax.rsqrt(var + LN_EPS) * g + b


def _route(logits_t, bias_ref):
    lg = [logits_t[e:e + 1, :] for e in range(N_EXPERTS)]
    mx = functools.reduce(jnp.maximum, lg)
    ex = [jnp.exp(v - mx) for v in lg]
    den = functools.reduce(jnp.add, ex)
    probs = [v / den for v in ex]
    sel = [probs[e] + bias_ref[e:e + 1, :] for e in range(N_EXPERTS)]

    group_score = []
    for g in range(N_GROUPS):
        a = sel[g * EXPERTS_PER_GROUP:(g + 1) * EXPERTS_PER_GROUP]
        pair_sums = [a[i] + a[j] for i in range(EXPERTS_PER_GROUP) for j in range(i + 1, EXPERTS_PER_GROUP)]
        group_score.append(functools.reduce(jnp.maximum, pair_sums))
    best = group_score[0]
    gidx = jnp.zeros(best.shape, jnp.int32)
    for g in range(1, N_GROUPS):
        better = group_score[g] > best
        gidx = jnp.where(better, g, gidx)
        best = jnp.where(better, group_score[g], best)

    def in_group(vals):
        out = []
        for j in range(EXPERTS_PER_GROUP):
            v = vals[(N_GROUPS - 1) * EXPERTS_PER_GROUP + j]
            for g in range(N_GROUPS - 2, -1, -1):
                v = jnp.where(gidx == g, vals[g * EXPERTS_PER_GROUP + j], v)
            out.append(v)
        return out

    s_in = in_group(sel)
    p_in = in_group(probs)
    i1 = jnp.zeros(best.shape, jnp.int32)
    v1 = s_in[0]
    for j in range(1, EXPERTS_PER_GROUP):
        better = s_in[j] > v1
        i1 = jnp.where(better, j, i1)
        v1 = jnp.where(better, s_in[j], v1)
    i2 = jnp.full(best.shape, -1, jnp.int32)
    v2 = jnp.full(best.shape, -jnp.inf, F32)
    for j in range(EXPERTS_PER_GROUP):
        better = (i1 != j) & (s_in[j] > v2)
        i2 = jnp.where(better, j, i2)
        v2 = jnp.where(better, s_in[j], v2)
    w1 = functools.reduce(jnp.add, [jnp.where(i1 == j, p_in[j], 0.0) for j in range(EXPERTS_PER_GROUP)])
    w2 = functools.reduce(jnp.add, [jnp.where(i2 == j, p_in[j], 0.0) for j in range(EXPERTS_PER_GROUP)])
    wsum = w1 + w2
    w1 = w1 / wsum
    w2 = w2 / wsum
    gates = []
    for e in range(N_EXPERTS):
        g, j = divmod(e, EXPERTS_PER_GROUP)
        in_g = gidx == g
        gates.append(jnp.where(in_g & (i1 == j), w1, 0.0) + jnp.where(in_g & (i2 == j), w2, 0.0))
    return gates


def _post_kernel(ysb_ref, ydf_ref, ycv_ref, h_ref, wo_ref, g1_ref, b1_ref, rw_ref, rb_ref,
                 wg_ref, wu_ref, wd_ref, g2_ref, b2_ref, o_ref, gate_ref):
    @pl.when(pl.program_id(0) == 0)
    def _():
        gate_ref[...] = jnp.zeros_like(gate_ref)

    mix = (_dot(ysb_ref[...], wo_ref[0:ATT_W, :])
           + _dot(ydf_ref[...], wo_ref[ATT_W:2 * ATT_W, :])
           + _dot(ycv_ref[...], wo_ref[2 * ATT_W:, :]))
    h1 = _layer_norm(ALPHA * h_ref[...] + mix, g1_ref[...], b1_ref[...])

    x_hi = h1.astype(BF16)
    x_lo = (h1 - x_hi.astype(F32)).astype(BF16)
    logits_t = _dot_nt(rw_ref[0], x_hi) + _dot_nt(rw_ref[1], x_hi) + _dot_nt(rw_ref[0], x_lo)
    gates = _route(logits_t, rb_ref)
    for e in range(N_EXPERTS):
        gate_ref[e:e + 1, :] = gates[e]
    gate_cols = gate_ref[...].T

    ffn = jnp.zeros(h1.shape, F32)
    for c in range(FF_W // FF_CHUNK):
        cs = slice(c * FF_CHUNK, (c + 1) * FF_CHUNK)
        gate_act = _dot(x_hi, wg_ref[:, cs])
        up = _dot(x_hi, wu_ref[:, cs])
        hid = gate_act * (1.0 / (1.0 + jnp.exp(-gate_act))) * up
        parts = []
        for k in range(FF_CHUNK // D_FF_EXPERT):
            e = c * (FF_CHUNK // D_FF_EXPERT) + k
            parts.append(hid[:, k * D_FF_EXPERT:(k + 1) * D_FF_EXPERT] * gate_cols[:, e:e + 1])
        ffn = ffn + _dot(jnp.concatenate(parts, axis=1).astype(BF16), wd_ref[cs, :])
    o_ref[...] = _layer_norm(ALPHA * h1 + ffn, g2_ref[...], b2_ref[...])


def _post(ysb, ydf, ycv, h, wo, g1, b1, rw, rb, wg, wu, wd, g2, b2):
    t = h.shape[0]
    tm = 512 if t % 512 == 0 else TQ
    row = lambda w: pl.BlockSpec((tm, w), lambda i: (i, 0))
    full = lambda shape: pl.BlockSpec(shape, lambda i: (0,) * len(shape))
    return pl.pallas_call(
        _post_kernel,
        grid=(t // tm,),
        in_specs=[row(ATT_W), row(ATT_W), row(CONV_W), row(D_MODEL),
                  full((D_MODEL, D_MODEL)), full((1, D_MODEL)), full((1, D_MODEL)),
                  full((2, N_EXPERTS, D_MODEL)), full((N_EXPERTS, 1)),
                  full((D_MODEL, FF_W)), full((D_MODEL, FF_W)), full((FF_W, D_MODEL)),
                  full((1, D_MODEL)), full((1, D_MODEL))],
        out_specs=row(D_MODEL),
        out_shape=jax.ShapeDtypeStruct((t, D_MODEL), F32),
        scratch_shapes=[pltpu.VMEM((128, tm), F32)],
        compiler_params=pltpu.CompilerParams(
            dimension_semantics=("arbitrary",), vmem_limit_bytes=VMEM_LIMIT),
        name="out_proj_moe",
    )(ysb, ydf, ycv, h, wo, g1, b1, rw, rb, wg, wu, wd, g2, b2)


def _rope_tables(lp):
    half = ROPE_DIM // 2
    pos = (jnp.arange(lp) - PAD_FRONT).astype(F32)
    inv = jnp.power(jnp.float32(ROPE_THETA), -jnp.arange(0, ROPE_DIM, 2, dtype=F32) / ROPE_DIM)
    ang = pos[:, None] * inv[None, :]
    cos, sin = jnp.cos(ang), jnp.sin(ang)
    pad = jnp.zeros((lp, DIFF_SUB - ROPE_DIM), F32)
    zero = jnp.zeros((lp, half), F32)
    rc = jnp.concatenate([cos, cos, pad + 1.0], axis=1)
    rm = jnp.concatenate([-sin, zero, pad], axis=1)
    rp = jnp.concatenate([zero, sin, pad], axis=1)
    reps = ATT_W // DIFF_SUB
    return tuple(jnp.tile(t, (1, reps)) for t in (rc, rm, rp))


def kernel(x, meta_tokens, w_in, conv_w, lambda_q1, lambda_k1, lambda_q2, lambda_k2, diff_norm_g, w_out,
           ln1_g, ln1_b, router_w, router_bias, w_gate, w_up, w_down, ln2_g, ln2_b):
    b, seq, _ = x.shape
    length = BLOCK + seq
    lp = -(-length // TQ) * TQ
    lead = jnp.concatenate([jnp.zeros((PAD_FRONT, D_MODEL), x.dtype), meta_tokens.astype(x.dtype)], axis=0)
    h = jnp.concatenate([jnp.broadcast_to(lead[None], (b, BLOCK, D_MODEL)), x,
                         jnp.zeros((b, lp - length, D_MODEL), x.dtype)], axis=1)
    rc, rm, rp = _rope_tables(lp)

    rw_t = router_w.T.astype(F32)
    rw_hi = rw_t.astype(BF16)
    rw_lo = (rw_t - rw_hi.astype(F32)).astype(BF16)
    rw = jnp.stack([rw_hi, rw_lo])
    rb = router_bias.astype(F32).reshape(N_EXPERTS, 1)

    for l in range(DEPTH):
        lam_init = jnp.full((1, 1), 0.8 - 0.6 * math.exp(-0.3 * l), F32)
        lamp = jnp.stack([lambda_q1[l], lambda_k1[l], lambda_q2[l], lambda_k2[l]]).astype(F32)
        sq_t, sk, sv_t, dq_t, dk, dv_t, y_conv = _in_proj(
            h, w_in[l].astype(BF16), rc, rm, rp, conv_w[l].astype(F32))
        y_sb = _sb_attention(sq_t, sk, sv_t)
        y_diff = _diff_attention(dq_t, dk, dv_t, lamp, lam_init, diff_norm_g[l].astype(F32).reshape(HEAD_DIM, 1))
        wg = w_gate[l].transpose(1, 0, 2).reshape(D_MODEL, FF_W).astype(BF16)
        wu = w_up[l].transpose(1, 0, 2).reshape(D_MODEL, FF_W).astype(BF16)
        wd = w_down[l].reshape(FF_W, D_MODEL).astype(BF16)
        flat = lambda a: a.reshape(b * lp, a.shape[-1])
        h = _post(flat(y_sb), flat(y_diff), flat(y_conv), flat(h), w_out[l].astype(BF16),
                  ln1_g[l].reshape(1, D_MODEL), ln1_b[l].reshape(1, D_MODEL), rw, rb, wg, wu, wd,
                  ln2_g[l].reshape(1, D_MODEL), ln2_b[l].reshape(1, D_MODEL)).reshape(b, lp, D_MODEL)
    return h[:, BLOCK:length]
```

```python
import functools
import math

import jax
import jax.numpy as jnp
from jax import lax
from jax.experimental import pallas as pl
from jax.experimental.pallas import tpu as pltpu

D_MODEL = 1024
DEPTH = 4
N_META = 16
BLOCK = 128
PAD_FRONT = BLOCK - N_META
HEAD_DIM = 64
N_HEADS = 4
ATT_W = N_HEADS * HEAD_DIM
CONV_W = 512
IN_COLS = 6 * ATT_W + 3 * CONV_W
DIFF_SUB = HEAD_DIM // 2
ROPE_DIM = DIFF_SUB // 4
ROPE_THETA = 500000.0
CONV_K = 3
N_EXPERTS = 16
N_GROUPS = 4
EXPERTS_PER_GROUP = N_EXPERTS // N_GROUPS
D_FF_EXPERT = 128
FF_W = N_EXPERTS * D_FF_EXPERT
FF_CHUNK = 512
ALPHA = (2.0 * DEPTH) ** 0.25
LN_EPS = 1e-5
NEG_BIG = -(2.0 ** 100)

TQ = 256
KB = 256
CONV_HALO = 8

SUM_ROWS = 16
V_ROWS = HEAD_DIM + SUM_ROWS
LOG2E = math.log2(math.e)
SIGN_BIT = -2 ** 31
UNDERFLOW_LOG2 = 160.0

F32 = jnp.float32
BF16 = jnp.bfloat16
VMEM_LIMIT = 56 * 1024 * 1024


def _dot(a, b):
    return jnp.dot(a, b, preferred_element_type=F32)


def _dot_nt(a, b):
    return lax.dot_general(a, b, (((1,), (1,)), ((), ())), preferred_element_type=F32)


def _in_proj_kernel(h_ref, w_ref, rc_ref, rm_ref, rp_ref, cw_ref,
                    sqT_ref, sk_ref, svT_ref, dqT_ref, dk_ref, dvT_ref, yc_ref, ubuf_ref, *, tm):
    j = pl.program_id(1)
    x = h_ref[0].astype(BF16)

    def proj(c0, width):
        return _dot(x, w_ref[:, c0:c0 + width])

    def store_t(o_ref, val):
        for s in range(tm // TQ):
            o_ref[0, s] = val[s * TQ:(s + 1) * TQ, :].T.astype(o_ref.dtype)

    def rope(t):
        halves = []
        for half in range(ATT_W // 128):
            sl = slice(half * 128, (half + 1) * 128)
            th = t[:, sl]
            halves.append(th * rc_ref[:, sl]
                          + pltpu.roll(th, 128 - ROPE_DIM // 2, axis=1) * rm_ref[:, sl]
                          + pltpu.roll(th, ROPE_DIM // 2, axis=1) * rp_ref[:, sl])
        return jnp.concatenate(halves, axis=1)

    store_t(sqT_ref, proj(0, ATT_W) * (HEAD_DIM ** -0.5 * LOG2E))
    sk_ref[0] = proj(ATT_W, ATT_W).astype(BF16)
    store_t(svT_ref, proj(2 * ATT_W, ATT_W))
    store_t(dqT_ref, rope(proj(3 * ATT_W, ATT_W)) * (DIFF_SUB ** -0.5 * LOG2E))
    dk_ref[0] = rope(proj(4 * ATT_W, ATT_W)).astype(BF16)
    dv = proj(5 * ATT_W, ATT_W)
    ones = jnp.ones((V_ROWS - HEAD_DIM, TQ), BF16)
    for s in range(tm // TQ):
        dv_t = dv[s * TQ:(s + 1) * TQ, :].T.astype(BF16)
        for h in range(N_HEADS):
            dvT_ref[0, s, h * V_ROWS:h * V_ROWS + HEAD_DIM, :] = dv_t[h * HEAD_DIM:(h + 1) * HEAD_DIM, :]
            dvT_ref[0, s, h * V_ROWS + HEAD_DIM:(h + 1) * V_ROWS, :] = ones

    c0 = 6 * ATT_W
    cb = proj(c0, CONV_W)
    cc = proj(c0 + CONV_W, CONV_W)
    ch = proj(c0 + 2 * CONV_W, CONV_W)
    pos = j * tm + lax.broadcasted_iota(jnp.int32, (tm, CONV_W), 0)
    u = jnp.where(pos >= PAD_FRONT, cc * ch, 0.0)

    @pl.when(j == 0)
    def _():
        ubuf_ref[0:CONV_HALO, :] = jnp.zeros((CONV_HALO, CONV_W), F32)

    ubuf_ref[CONV_HALO:CONV_HALO + tm, :] = u
    u1 = ubuf_ref[CONV_HALO - 1:CONV_HALO - 1 + tm, :]
    u2 = ubuf_ref[CONV_HALO - 2:CONV_HALO - 2 + tm, :]
    conv = cw_ref[0:1, :] * u2 + cw_ref[1:2, :] * u1 + cw_ref[2:3, :] * u
    yc_ref[0] = (cb * conv).astype(BF16)
    ubuf_ref[0:CONV_HALO, :] = ubuf_ref[tm:tm + CONV_HALO, :]


def _in_proj(h, w, rc, rm, rp, cw):
    b, lp, _ = h.shape
    tm = 768 if lp % 768 == 0 else TQ
    nj = lp // tm
    nq = lp // TQ
    t_shape = jax.ShapeDtypeStruct((b, nq, ATT_W, TQ), BF16)
    n_shape = jax.ShapeDtypeStruct((b, lp, ATT_W), BF16)
    v_shape = jax.ShapeDtypeStruct((b, nq, N_HEADS * V_ROWS, TQ), BF16)
    t_spec = pl.BlockSpec((1, tm // TQ, ATT_W, TQ), lambda bi, j: (bi, j, 0, 0))
    v_spec = pl.BlockSpec((1, tm // TQ, N_HEADS * V_ROWS, TQ), lambda bi, j: (bi, j, 0, 0))
    n_spec = pl.BlockSpec((1, tm, ATT_W), lambda bi, j: (bi, j, 0))
    tab_spec = pl.BlockSpec((tm, ATT_W), lambda bi, j: (j, 0))
    return pl.pallas_call(
        functools.partial(_in_proj_kernel, tm=tm),
        grid=(b, nj),
        in_specs=[
            pl.BlockSpec((1, tm, D_MODEL), lambda bi, j: (bi, j, 0)),
            pl.BlockSpec((D_MODEL, IN_COLS), lambda bi, j: (0, 0)),
            tab_spec, tab_spec, tab_spec,
            pl.BlockSpec((CONV_K, CONV_W), lambda bi, j: (0, 0)),
        ],
        out_specs=[t_spec, n_spec, t_spec, t_spec, n_spec, v_spec,
                   pl.BlockSpec((1, tm, CONV_W), lambda bi, j: (bi, j, 0))],
        out_shape=[t_shape, n_shape, t_shape, t_shape, n_shape, v_shape,
                   jax.ShapeDtypeStruct((b, lp, CONV_W), BF16)],
        scratch_shapes=[pltpu.VMEM((tm + CONV_HALO, CONV_W), F32)],
        compiler_params=pltpu.CompilerParams(
            dimension_semantics=("arbitrary", "arbitrary"), vmem_limit_bytes=VMEM_LIMIT),
        name="in_proj",
    )(h, w, rc, rm, rp, cw)


def _store_masked_rows(qm_ref, q_t, width):
    frow = lax.broadcasted_iota(jnp.int32, q_t.shape, 0)
    zero = jnp.zeros_like(q_t)
    for i in range(ATT_W // width):
        qm_ref[i] = jnp.where((frow >= i * width) & (frow < (i + 1) * width), q_t, zero)


def _causal_sweep(qi, stage_a, stage_b, stage_ba, rest_is_zero=None):
    odd = qi & 1
    stage_a(qi, odd, "diag")

    @pl.when((odd == 0) & (qi >= 2))
    def _():
        stage_ba(qi, 0, None)

    kb_odd = qi - 1 + odd
    n_pairs = jnp.maximum(kb_odd - 1, 0) // 2

    if rest_is_zero is None:
        def pair(kb):
            stage_ba(kb, 1, None)
            stage_ba(kb - 1, 0, None)

        def body(j, carry):
            pair(kb_odd - 4 * j)
            pair(kb_odd - 4 * j - 2)
            return carry

        lax.fori_loop(0, n_pairs // 2, body, 0)

        @pl.when(n_pairs % 2 == 1)
        def _():
            pair(kb_odd - 2 * (n_pairs - 1))

        @pl.when(qi >= 1)
        def _():
            stage_ba(1, 1, "first")

        stage_b(0, 0)
    else:
        def cond(carry):
            j, stop = carry
            return (j < n_pairs) & jnp.logical_not(stop)

        def body(carry):
            j, _ = carry
            kb = kb_odd - 2 * j
            stage_ba(kb, 1, None)

            @pl.when(jnp.logical_not(rest_is_zero()))
            def _():
                stage_ba(kb - 1, 0, None)

            return j + 1, rest_is_zero()

        _, stop = lax.while_loop(cond, body, (jnp.int32(0), rest_is_zero()))

        @pl.when((qi >= 1) & jnp.logical_not(stop))
        def _():
            stage_ba(1, 1, "first")

        @pl.when(jnp.logical_not(rest_is_zero()))
        def _():
            stage_b(0, 0)


def _key_mask(kind, qi, kb, inclusive):
    key_idx = kb * KB + lax.broadcasted_iota(jnp.int32, (KB, TQ), 0)
    valid = key_idx >= PAD_FRONT
    if kind == "diag":
        q_idx = qi * TQ + lax.broadcasted_iota(jnp.int32, (KB, TQ), 1)
        valid = valid & ((key_idx <= q_idx) if inclusive else (key_idx < q_idx))
    return valid


def _att_specs(b, lp, v_rows):
    nq = lp // TQ
    nkb = lp // KB
    in_specs = [
        pl.BlockSpec((1, 1, ATT_W, TQ), lambda bi, qi: (bi, qi, 0, 0)),
        pl.BlockSpec((1, lp, ATT_W), lambda bi, qi: (bi, 0, 0)),
        pl.BlockSpec((1, nkb, v_rows, KB), lambda bi, qi: (bi, 0, 0, 0)),
    ]
    out_spec = pl.BlockSpec((1, TQ, ATT_W), lambda bi, qi: (bi, qi, 0))
    out_shape = jax.ShapeDtypeStruct((b, lp, ATT_W), BF16)
    return (b, nq), in_specs, out_spec, out_shape


def _sb_kernel(qT_ref, k_ref, vT_ref, o_ref, qm_ref, upper_ref, lb_ref, lr_ref, acc_ref, later_ref):
    qi = pl.program_id(1)
    _store_masked_rows(qm_ref, qT_ref[0, 0], HEAD_DIM)
    @pl.when((pl.program_id(0) == 0) & (qi == 0))
    def _():
        si = lax.broadcasted_iota(jnp.int32, (KB, KB), 0)
        ji = lax.broadcasted_iota(jnp.int32, (KB, KB), 1)
        upper_ref[0:KB, :] = (ji > si).astype(BF16)
        upper_ref[KB:, :] = jnp.ones((SUM_ROWS, KB), BF16)

    acc_ref[...] = jnp.zeros_like(acc_ref)
    later_ref[...] = jnp.zeros_like(later_ref)

    def a_head(h, kb, slot, valid):
        kblk = k_ref[0, pl.ds(pl.multiple_of(kb * KB, KB), KB), :]
        z = _dot(kblk, qm_ref[h])
        if valid is not None:
            z = jnp.where(valid, z, NEG_BIG)
        neg_abs = lax.bitcast_convert_type(lax.bitcast_convert_type(z, jnp.int32) | SIGN_BIT, F32)
        sp = jnp.log(1.0 + jnp.exp2(neg_abs)) * LOG2E
        log_beta = jnp.minimum(z, 0.0) - sp
        lb_ref[slot * N_HEADS + h] = log_beta
        lr_ref[slot * N_HEADS + h] = (log_beta - z).astype(BF16)

    def b_intra(h, slot):
        return _dot(upper_ref[...], lr_ref[slot * N_HEADS + h])

    def b_value(h, kb, slot, sums):
        hs = slice(h * HEAD_DIM, (h + 1) * HEAD_DIM)
        w = jnp.exp2(lb_ref[slot * N_HEADS + h] + sums[0:KB, :])
        later = later_ref[h:h + 1, :]
        acc_ref[hs, :] += _dot(vT_ref[0, kb, hs, :], w.astype(BF16)) * jnp.exp2(later)
        later_ref[h:h + 1, :] = later + sums[KB:KB + 1, :]

    def stage_a(kb, slot, kind):
        valid = _key_mask(kind, qi, kb, inclusive=False)
        for h in range(N_HEADS):
            a_head(h, kb, slot, valid)

    def stage_b(kb, slot):
        intra = [b_intra(h, slot) for h in range(N_HEADS)]
        for h in range(N_HEADS):
            b_value(h, kb, slot, intra[h])

    def stage_ba(kb, slot, kind):
        valid = None if kind is None else _key_mask(kind, qi, kb - 1, inclusive=False)
        intra = {}
        for h in range(N_HEADS + 1):
            if h < N_HEADS:
                intra[h] = b_intra(h, slot)
                a_head(h, kb - 1, 1 - slot, valid)
            if h >= 1:
                b_value(h - 1, kb, slot, intra.pop(h - 1))

    def rest_is_zero():
        return jnp.max(later_ref[0:N_HEADS, :]) < -UNDERFLOW_LOG2

    _causal_sweep(qi, stage_a, stage_b, stage_ba, rest_is_zero)
    o_ref[0] = acc_ref[...].T.astype(o_ref.dtype)


def _sb_attention(q_t, k, v_t):
    b, lp, _ = k.shape
    grid, in_specs, out_spec, out_shape = _att_specs(b, lp, ATT_W)
    return pl.pallas_call(
        _sb_kernel,
        grid=grid,
        in_specs=in_specs,
        out_specs=out_spec,
        out_shape=out_shape,
        scratch_shapes=[pltpu.VMEM((N_HEADS, ATT_W, TQ), BF16), pltpu.VMEM((KB + SUM_ROWS, KB), BF16),
                        pltpu.VMEM((2 * N_HEADS, KB, TQ), F32), pltpu.VMEM((2 * N_HEADS, KB, TQ), BF16),
                        pltpu.VMEM((ATT_W, TQ), F32), pltpu.VMEM((8, TQ), F32)],
        compiler_params=pltpu.CompilerParams(
            dimension_semantics=("arbitrary", "arbitrary"), vmem_limit_bytes=VMEM_LIMIT),
        name="sb_attention",
    )(q_t, k, v_t)


def _diff_kernel(qT_ref, k_ref, vT_ref, lamp_ref, li_ref, g_ref, o_ref,
                 qm_ref, s_ref, bmax_ref, acc_ref, m_ref):
    qi = pl.program_id(1)
    _store_masked_rows(qm_ref, qT_ref[0, 0], DIFF_SUB)
    acc_ref[...] = jnp.zeros_like(acc_ref)
    m_ref[...] = jnp.full_like(m_ref, NEG_BIG)
    n_maps = 2 * N_HEADS

    def a_map(i, kb, slot, valid):
        kblk = k_ref[0, pl.ds(pl.multiple_of(kb * KB, KB), KB), :]
        s = _dot(kblk, qm_ref[i])
        if valid is not None:
            s = jnp.where(valid, s, NEG_BIG)
        s_ref[slot * n_maps + i] = s.astype(BF16)
        bmax_ref[slot, i:i + 1, :] = jnp.max(s, axis=0, keepdims=True)

    def b_map(i, kb, slot):
        vh = vT_ref[0, kb, (i // 2) * V_ROWS:(i // 2 + 1) * V_ROWS, :]
        m_old = m_ref[i:i + 1, :]
        m_new = jnp.maximum(m_old, bmax_ref[slot, i:i + 1, :]).astype(BF16)
        alpha = jnp.exp2(m_old - m_new.astype(F32))
        p = jnp.exp2(s_ref[slot * n_maps + i] - m_new)
        acc_ref[i] = alpha * acc_ref[i] + _dot(vh, p)
        m_ref[i:i + 1, :] = m_new.astype(F32)

    def stage_a(kb, slot, kind):
        valid = _key_mask(kind, qi, kb, inclusive=True)
        for i in range(n_maps):
            a_map(i, kb, slot, valid)

    def stage_b(kb, slot):
        for i in range(n_maps):
            b_map(i, kb, slot)

    def stage_ba(kb, slot, kind):
        valid = None if kind is None else _key_mask(kind, qi, kb - 1, inclusive=True)
        for i in range(n_maps):
            a_map(i, kb - 1, 1 - slot, valid)
            b_map(i, kb, slot)

    _causal_sweep(qi, stage_a, stage_b, stage_ba)

    lp = lamp_ref[...]
    lam_init = li_ref[...]
    lam = (jnp.exp(jnp.sum(lp[0:1] * lp[1:2], axis=1, keepdims=True))
           - jnp.exp(jnp.sum(lp[2:3] * lp[3:4], axis=1, keepdims=True)) + lam_init)
    outs = []
    for h in range(N_HEADS):
        a1 = acc_ref[2 * h]
        a2 = acc_ref[2 * h + 1]
        o = (a1[:HEAD_DIM] / a1[HEAD_DIM:HEAD_DIM + 1]
             - lam * (a2[:HEAD_DIM] / a2[HEAD_DIM:HEAD_DIM + 1]))
        ms = jnp.mean(o * o, axis=0, keepdims=True)
        outs.append(o * lax.rsqrt(ms + LN_EPS) * g_ref[...] * (1.0 - lam_init))
    o_ref[0] = jnp.concatenate(outs, axis=0).T.astype(o_ref.dtype)


def _diff_attention(q_t, k, v_t, lamp, lam_init, g):
    b, lp, _ = k.shape
    grid, in_specs, out_spec, out_shape = _att_specs(b, lp, N_HEADS * V_ROWS)
    in_specs = in_specs + [
        pl.BlockSpec((4, DIFF_SUB), lambda bi, qi: (0, 0)),
        pl.BlockSpec((1, 1), lambda bi, qi: (0, 0)),
        pl.BlockSpec((HEAD_DIM, 1), lambda bi, qi: (0, 0)),
    ]
    return pl.pallas_call(
        _diff_kernel,
        grid=grid,
        in_specs=in_specs,
        out_specs=out_spec,
        out_shape=out_shape,
        scratch_shapes=[pltpu.VMEM((2 * N_HEADS, ATT_W, TQ), BF16), pltpu.VMEM((4 * N_HEADS, KB, TQ), BF16),
                        pltpu.VMEM((2, 8, TQ), F32),
                        pltpu.VMEM((2 * N_HEADS, V_ROWS, TQ), F32), pltpu.VMEM((8, TQ), F32)],
        compiler_params=pltpu.CompilerParams(
            dimension_semantics=("arbitrary", "arbitrary"), vmem_limit_bytes=VMEM_LIMIT),
        name="diff_attention",
    )(q_t, k, v_t, lamp, lam_init, g)


def _layer_norm(x, g, b):
    mu = jnp.mean(x, axis=-1, keepdims=True)
    xc = x - mu
    var = jnp.mean(xc * xc, axis=-1, keepdims=True)
    return xc * lax.rsqrt(var + LN_EPS) * g + b


def _route(logits_t, bias_ref):
    lg = [logits_t[e:e + 1, :] for e in range(N_EXPERTS)]
    mx = functools.reduce(jnp.maximum, lg)
    ex = [jnp.exp(v - mx) for v in lg]
    den = functools.reduce(jnp.add, ex)
    probs = [v / den for v in ex]
    sel = [probs[e] + bias_ref[e:e + 1, :] for e in range(N_EXPERTS)]

    group_score = []
    for g in range(N_GROUPS):
        a = sel[g * EXPERTS_PER_GROUP:(g + 1) * EXPERTS_PER_GROUP]
        pair_sums = [a[i] + a[j] for i in range(EXPERTS_PER_GROUP) for j in range(i + 1, EXPERTS_PER_GROUP)]
        group_score.append(functools.reduce(jnp.maximum, pair_sums))
    best = group_score[0]
    gidx = jnp.zeros(best.shape, jnp.int32)
    for g in range(1, N_GROUPS):
        better = group_score[g] > best
        gidx = jnp.where(better, g, gidx)
        best = jnp.where(better, group_score[g], best)

    def in_group(vals):
        out = []
        for j in range(EXPERTS_PER_GROUP):
            v = vals[(N_GROUPS - 1) * EXPERTS_PER_GROUP + j]
            for g in range(N_GROUPS - 2, -1, -1):
                v = jnp.where(gidx == g, vals[g * EXPERTS_PER_GROUP + j], v)
            out.append(v)
        return out

    s_in = in_group(sel)
    p_in = in_group(probs)
    i1 = jnp.zeros(best.shape, jnp.int32)
    v1 = s_in[0]
    for j in range(1, EXPERTS_PER_GROUP):
        better = s_in[j] > v1
        i1 = jnp.where(better, j, i1)
        v1 = jnp.where(better, s_in[j], v1)
    i2 = jnp.full(best.shape, -1, jnp.int32)
    v2 = jnp.full(best.shape, -jnp.inf, F32)
    for j in range(EXPERTS_PER_GROUP):
        better = (i1 != j) & (s_in[j] > v2)
        i2 = jnp.where(better, j, i2)
        v2 = jnp.where(better, s_in[j], v2)
    w1 = functools.reduce(jnp.add, [jnp.where(i1 == j, p_in[j], 0.0) for j in range(EXPERTS_PER_GROUP)])
    w2 = functools.reduce(jnp.add, [jnp.where(i2 == j, p_in[j], 0.0) for j in range(EXPERTS_PER_GROUP)])
    wsum = w1 + w2
    w1 = w1 / wsum
    w2 = w2 / wsum
    gates = []
    for e in range(N_EXPERTS):
        g, j = divmod(e, EXPERTS_PER_GROUP)
        in_g = gidx == g
        gates.append(jnp.where(in_g & (i1 == j), w1, 0.0) + jnp.where(in_g & (i2 == j), w2, 0.0))
    return gates


def _post_kernel(ysb_ref, ydf_ref, ycv_ref, h_ref, wo_ref, g1_ref, b1_ref, rw_ref, rb_ref,
                 wg_ref, wu_ref, wd_ref, g2_ref, b2_ref, o_ref, gate_ref):
    @pl.when(pl.program_id(0) == 0)
    def _():
        gate_ref[...] = jnp.zeros_like(gate_ref)

    hm = h_ref.shape[0] // 2
    halves = (slice(0, hm), slice(hm, 2 * hm))
    n_chunks = FF_W // FF_CHUNK

    def out_proj(r):
        return (_dot(ysb_ref[r, :], wo_ref[0:ATT_W, :])
                + _dot(ydf_ref[r, :], wo_ref[ATT_W:2 * ATT_W, :])
                + _dot(ycv_ref[r, :], wo_ref[2 * ATT_W:, :]))

    def norm_route(i, mix):
        h1 = _layer_norm(ALPHA * h_ref[halves[i], :] + mix, g1_ref[...], b1_ref[...])
        x_hi = h1.astype(BF16)
        x_lo = (h1 - x_hi.astype(F32)).astype(BF16)
        logits_t = _dot_nt(rw_ref[0], x_hi) + _dot_nt(rw_ref[1], x_hi) + _dot_nt(rw_ref[0], x_lo)
        gates = _route(logits_t, rb_ref)
        for e in range(N_EXPERTS):
            gate_ref[i, e:e + 1, :] = gates[e]
        return h1, x_hi, gate_ref[i].T

    def expert_chunk(c, x_hi, gate_cols):
        cs = slice(c * FF_CHUNK, (c + 1) * FF_CHUNK)
        gate_act = _dot(x_hi, wg_ref[:, cs])
        up = _dot(x_hi, wu_ref[:, cs])
        hid = gate_act * (1.0 / (1.0 + jnp.exp(-gate_act))) * up
        parts = []
        for k in range(FF_CHUNK // D_FF_EXPERT):
            e = c * (FF_CHUNK // D_FF_EXPERT) + k
            parts.append(hid[:, k * D_FF_EXPERT:(k + 1) * D_FF_EXPERT] * gate_cols[:, e:e + 1])
        return _dot(jnp.concatenate(parts, axis=1).astype(BF16), wd_ref[cs, :])

    mix_a = out_proj(halves[0])
    mix_b = out_proj(halves[1])
    h1_a, x_a, gates_a = norm_route(0, mix_a)
    ffn_a = expert_chunk(0, x_a, gates_a) + expert_chunk(1, x_a, gates_a)
    h1_b, x_b, gates_b = norm_route(1, mix_b)
    for c in range(2, n_chunks):
        ffn_a = ffn_a + expert_chunk(c, x_a, gates_a)
    o_ref[halves[0], :] = _layer_norm(ALPHA * h1_a + ffn_a, g2_ref[...], b2_ref[...])
    ffn_b = expert_chunk(0, x_b, gates_b)
    for c in range(1, n_chunks):
        ffn_b = ffn_b + expert_chunk(c, x_b, gates_b)
    o_ref[halves[1], :] = _layer_norm(ALPHA * h1_b + ffn_b, g2_ref[...], b2_ref[...])


def _post(ysb, ydf, ycv, h, wo, g1, b1, rw, rb, wg, wu, wd, g2, b2):
    t = h.shape[0]
    tm = 1024 if t % 1024 == 0 else 2 * TQ
    row = lambda w: pl.BlockSpec((tm, w), lambda i: (i, 0))
    full = lambda shape: pl.BlockSpec(shape, lambda i: (0,) * len(shape), pipeline_mode=pl.Buffered(1))
    return pl.pallas_call(
        _post_kernel,
        grid=(t // tm,),
        in_specs=[row(ATT_W), row(ATT_W), row(CONV_W), row(D_MODEL),
                  full((D_MODEL, D_MODEL)), full((1, D_MODEL)), full((1, D_MODEL)),
                  full((2, N_EXPERTS, D_MODEL)), full((N_EXPERTS, 1)),
                  full((D_MODEL, FF_W)), full((D_MODEL, FF_W)), full((FF_W, D_MODEL)),
                  full((1, D_MODEL)), full((1, D_MODEL))],
        out_specs=row(D_MODEL),
        out_shape=jax.ShapeDtypeStruct((t, D_MODEL), F32),
        scratch_shapes=[pltpu.VMEM((2, 128, tm // 2), F32)],
        compiler_params=pltpu.CompilerParams(
            dimension_semantics=("arbitrary",), vmem_limit_bytes=VMEM_LIMIT),
        name="out_proj_moe",
    )(ysb, ydf, ycv, h, wo, g1, b1, rw, rb, wg, wu, wd, g2, b2)


def _rope_tables(lp):
    half = ROPE_DIM // 2
    pos = (jnp.arange(lp) - PAD_FRONT).astype(F32)
    inv = jnp.power(jnp.float32(ROPE_THETA), -jnp.arange(0, ROPE_DIM, 2, dtype=F32) / ROPE_DIM)
    ang = pos[:, None] * inv[None, :]
    cos, sin = jnp.cos(ang), jnp.sin(ang)
    pad = jnp.zeros((lp, DIFF_SUB - ROPE_DIM), F32)
    zero = jnp.zeros((lp, half), F32)
    rc = jnp.concatenate([cos, cos, pad + 1.0], axis=1)
    rm = jnp.concatenate([-sin, zero, pad], axis=1)
    rp = jnp.concatenate([zero, sin, pad], axis=1)
    reps = ATT_W // DIFF_SUB
    return tuple(jnp.tile(t, (1, reps)) for t in (rc, rm, rp))


def kernel(x, meta_tokens, w_in, conv_w, lambda_q1, lambda_k1, lambda_q2, lambda_k2, diff_norm_g, w_out,
           ln1_g, ln1_b, router_w, router_bias, w_gate, w_up, w_down, ln2_g, ln2_b):
    b, seq, _ = x.shape
    length = BLOCK + seq
    lp = -(-length // TQ) * TQ
    lead = jnp.concatenate([jnp.zeros((PAD_FRONT, D_MODEL), x.dtype), meta_tokens.astype(x.dtype)], axis=0)
    h = jnp.concatenate([jnp.broadcast_to(lead[None], (b, BLOCK, D_MODEL)), x,
                         jnp.zeros((b, lp - length, D_MODEL), x.dtype)], axis=1)
    rc, rm, rp = _rope_tables(lp)

    rw_t = router_w.T.astype(F32)
    rw_hi = rw_t.astype(BF16)
    rw_lo = (rw_t - rw_hi.astype(F32)).astype(BF16)
    rw = jnp.stack([rw_hi, rw_lo])
    rb = router_bias.astype(F32).reshape(N_EXPERTS, 1)

    for l in range(DEPTH):
        lam_init = jnp.full((1, 1), 0.8 - 0.6 * math.exp(-0.3 * l), F32)
        lamp = jnp.stack([lambda_q1[l], lambda_k1[l], lambda_q2[l], lambda_k2[l]]).astype(F32)
        sq_t, sk, sv_t, dq_t, dk, dv_t, y_conv = _in_proj(
            h, w_in[l].astype(BF16), rc, rm, rp, conv_w[l].astype(F32))
        y_sb = _sb_attention(sq_t, sk, sv_t)
        y_diff = _diff_attention(dq_t, dk, dv_t, lamp, lam_init, diff_norm_g[l].astype(F32).reshape(HEAD_DIM, 1))
        wg = w_gate[l].transpose(1, 0, 2).reshape(D_MODEL, FF_W).astype(BF16)
        wu = w_up[l].transpose(1, 0, 2).reshape(D_MODEL, FF_W).astype(BF16)
        wd = w_down[l].reshape(FF_W, D_MODEL).astype(BF16)
        flat = lambda a: a.reshape(b * lp, a.shape[-1])
        h = _post(flat(y_sb), flat(y_diff), flat(y_conv), flat(h), w_out[l].astype(BF16),
                  ln1_g[l].reshape(1, D_MODEL), ln1_b[l].reshape(1, D_MODEL), rw, rb, wg, wu, wd,
                  ln2_g[l].reshape(1, D_MODEL), ln2_b[l].reshape(1, D_MODEL)).reshape(b, lp, D_MODEL)
    return h[:, BLOCK:length]
```

```python
import functools
import math

import jax
import jax.numpy as jnp
from jax import lax
from jax.experimental import pallas as pl
from jax.experimental.pallas import tpu as pltpu

D_MODEL = 1024
DEPTH = 4
N_META = 16
BLOCK = 128
PAD_FRONT = BLOCK - N_META
HEAD_DIM = 64
N_HEADS = 4
ATT_W = N_HEADS * HEAD_DIM
CONV_W = 512
IN_COLS = 6 * ATT_W + 3 * CONV_W
DIFF_SUB = HEAD_DIM // 2
ROPE_DIM = DIFF_SUB // 4
ROPE_THETA = 500000.0
CONV_K = 3
N_EXPERTS = 16
N_GROUPS = 4
EXPERTS_PER_GROUP = N_EXPERTS // N_GROUPS
D_FF_EXPERT = 128
FF_W = N_EXPERTS * D_FF_EXPERT
FF_CHUNK = 512
POST_PARTS = 2
ALPHA = (2.0 * DEPTH) ** 0.25
LN_EPS = 1e-5
NEG_BIG = -(2.0 ** 100)

TQ = 256
KB = 256
CONV_HALO = 8
SWEEP_UNROLL = 4

SUM_ROWS = 16
V_ROWS = HEAD_DIM + SUM_ROWS
LOG2E = math.log2(math.e)
SIGN_BIT = -2 ** 31
UNDERFLOW_LOG2 = 151.0

F32 = jnp.float32
BF16 = jnp.bfloat16
VMEM_LIMIT = 56 * 1024 * 1024


def _dot(a, b):
    return jnp.dot(a, b, preferred_element_type=F32)


def _dot_nt(a, b):
    return lax.dot_general(a, b, (((1,), (1,)), ((), ())), preferred_element_type=F32)


def _in_proj_kernel(h_ref, w_ref, rc_ref, rm_ref, rp_ref, cw_ref,
                    sqT_ref, sk_ref, svT_ref, dqT_ref, dk_ref, dvT_ref, yc_ref, ubuf_ref, *, tm):
    j = pl.program_id(1)
    x = h_ref[0].astype(BF16)

    def proj(c0, width):
        return _dot(x, w_ref[:, c0:c0 + width])

    def store_t(o_ref, val):
        for s in range(tm // TQ):
            o_ref[0, s] = val[s * TQ:(s + 1) * TQ, :].T.astype(o_ref.dtype)

    def rope(t):
        halves = []
        for half in range(ATT_W // 128):
            sl = slice(half * 128, (half + 1) * 128)
            th = t[:, sl]
            halves.append(th * rc_ref[:, sl]
                          + pltpu.roll(th, 128 - ROPE_DIM // 2, axis=1) * rm_ref[:, sl]
                          + pltpu.roll(th, ROPE_DIM // 2, axis=1) * rp_ref[:, sl])
        return jnp.concatenate(halves, axis=1)

    store_t(sqT_ref, proj(0, ATT_W) * (HEAD_DIM ** -0.5 * LOG2E))
    sk_ref[0] = proj(ATT_W, ATT_W).astype(BF16)
    store_t(svT_ref, proj(2 * ATT_W, ATT_W))
    store_t(dqT_ref, rope(proj(3 * ATT_W, ATT_W)) * (DIFF_SUB ** -0.5 * LOG2E))
    dk_ref[0] = rope(proj(4 * ATT_W, ATT_W)).astype(BF16)
    dv = proj(5 * ATT_W, ATT_W)
    ones = jnp.ones((V_ROWS - HEAD_DIM, TQ), BF16)
    for s in range(tm // TQ):
        dv_t = dv[s * TQ:(s + 1) * TQ, :].T.astype(BF16)
        for h in range(N_HEADS):
            dvT_ref[0, s, h * V_ROWS:h * V_ROWS + HEAD_DIM, :] = dv_t[h * HEAD_DIM:(h + 1) * HEAD_DIM, :]
            dvT_ref[0, s, h * V_ROWS + HEAD_DIM:(h + 1) * V_ROWS, :] = ones

    c0 = 6 * ATT_W
    cb = proj(c0, CONV_W)
    cc = proj(c0 + CONV_W, CONV_W)
    ch = proj(c0 + 2 * CONV_W, CONV_W)
    pos = j * tm + lax.broadcasted_iota(jnp.int32, (tm, CONV_W), 0)
    u = jnp.where(pos >= PAD_FRONT, cc * ch, 0.0)

    @pl.when(j == 0)
    def _():
        ubuf_ref[0:CONV_HALO, :] = jnp.zeros((CONV_HALO, CONV_W), F32)

    ubuf_ref[CONV_HALO:CONV_HALO + tm, :] = u
    u1 = ubuf_ref[CONV_HALO - 1:CONV_HALO - 1 + tm, :]
    u2 = ubuf_ref[CONV_HALO - 2:CONV_HALO - 2 + tm, :]
    conv = cw_ref[0:1, :] * u2 + cw_ref[1:2, :] * u1 + cw_ref[2:3, :] * u
    yc_ref[0] = (cb * conv).astype(BF16)
    ubuf_ref[0:CONV_HALO, :] = ubuf_ref[tm:tm + CONV_HALO, :]


def _in_proj(h, w, rc, rm, rp, cw):
    b, lp, _ = h.shape
    tm = 768 if lp % 768 == 0 else TQ
    nj = lp // tm
    nq = lp // TQ
    t_shape = jax.ShapeDtypeStruct((b, nq, ATT_W, TQ), BF16)
    n_shape = jax.ShapeDtypeStruct((b, lp, ATT_W), BF16)
    v_shape = jax.ShapeDtypeStruct((b, nq, N_HEADS * V_ROWS, TQ), BF16)
    t_spec = pl.BlockSpec((1, tm // TQ, ATT_W, TQ), lambda bi, j: (bi, j, 0, 0))
    v_spec = pl.BlockSpec((1, tm // TQ, N_HEADS * V_ROWS, TQ), lambda bi, j: (bi, j, 0, 0))
    n_spec = pl.BlockSpec((1, tm, ATT_W), lambda bi, j: (bi, j, 0))
    tab_spec = pl.BlockSpec((tm, ATT_W), lambda bi, j: (j, 0))
    return pl.pallas_call(
        functools.partial(_in_proj_kernel, tm=tm),
        grid=(b, nj),
        in_specs=[
            pl.BlockSpec((1, tm, D_MODEL), lambda bi, j: (bi, j, 0)),
            pl.BlockSpec((D_MODEL, IN_COLS), lambda bi, j: (0, 0)),
            tab_spec, tab_spec, tab_spec,
            pl.BlockSpec((CONV_K, CONV_W), lambda bi, j: (0, 0)),
        ],
        out_specs=[t_spec, n_spec, t_spec, t_spec, n_spec, v_spec,
                   pl.BlockSpec((1, tm, CONV_W), lambda bi, j: (bi, j, 0))],
        out_shape=[t_shape, n_shape, t_shape, t_shape, n_shape, v_shape,
                   jax.ShapeDtypeStruct((b, lp, CONV_W), BF16)],
        scratch_shapes=[pltpu.VMEM((tm + CONV_HALO, CONV_W), F32)],
        compiler_params=pltpu.CompilerParams(
            dimension_semantics=("arbitrary", "arbitrary"), vmem_limit_bytes=VMEM_LIMIT),
        name="in_proj",
    )(h, w, rc, rm, rp, cw)


def _store_masked_rows(qm_ref, q_t, width):
    frow = lax.broadcasted_iota(jnp.int32, q_t.shape, 0)
    zero = jnp.zeros_like(q_t)
    for i in range(ATT_W // width):
        qm_ref[i] = jnp.where((frow >= i * width) & (frow < (i + 1) * width), q_t, zero)


def _causal_sweep(qi, stage_a, stage_b, stage_ba, rest_is_zero=None):
    odd = qi & 1
    stage_a(qi, odd, "diag")

    @pl.when((odd == 0) & (qi >= 2))
    def _():
        stage_ba(qi, 0, None)

    kb_odd = qi - 1 + odd
    n_pairs = jnp.maximum(kb_odd - 1, 0) // 2

    if rest_is_zero is None:
        def pair(kb):
            stage_ba(kb, 1, None)
            stage_ba(kb - 1, 0, None)

        def body(j, carry):
            for u in range(SWEEP_UNROLL):
                pair(kb_odd - 2 * (SWEEP_UNROLL * j + u))
            return carry

        n_trips = n_pairs // SWEEP_UNROLL
        lax.fori_loop(0, n_trips, body, 0)

        def tail(p, carry):
            pair(kb_odd - 2 * p)
            return carry

        lax.fori_loop(SWEEP_UNROLL * n_trips, n_pairs, tail, 0)

        @pl.when(qi >= 1)
        def _():
            stage_ba(1, 1, "first")

        stage_b(0, 0)
    else:
        def cond(carry):
            j, stop = carry
            return (j < n_pairs) & jnp.logical_not(stop)

        def body(carry):
            j, _ = carry
            kb = kb_odd - 2 * j
            stage_ba(kb, 1, None)

            @pl.when(jnp.logical_not(rest_is_zero()))
            def _():
                stage_ba(kb - 1, 0, None)

            return j + 1, rest_is_zero()

        _, stop = lax.while_loop(cond, body, (jnp.int32(0), jnp.zeros((), jnp.bool_)))

        @pl.when(jnp.logical_not(stop))
        def _():
            @pl.when(qi >= 1)
            def _():
                stage_ba(1, 1, "first")

            stage_b(0, 0)


def _key_mask(kind, qi, kb, inclusive):
    key_idx = kb * KB + lax.broadcasted_iota(jnp.int32, (KB, TQ), 0)
    valid = key_idx >= PAD_FRONT
    if kind == "diag":
        q_idx = qi * TQ + lax.broadcasted_iota(jnp.int32, (KB, TQ), 1)
        valid = valid & ((key_idx <= q_idx) if inclusive else (key_idx < q_idx))
    return valid


def _att_specs(b, lp, v_rows):
    nq = lp // TQ
    nkb = lp // KB
    in_specs = [
        pl.BlockSpec((1, 1, ATT_W, TQ), lambda bi, qi: (bi, qi, 0, 0)),
        pl.BlockSpec((1, lp, ATT_W), lambda bi, qi: (bi, 0, 0)),
        pl.BlockSpec((1, nkb, v_rows, KB), lambda bi, qi: (bi, 0, 0, 0)),
    ]
    out_spec = pl.BlockSpec((1, TQ, ATT_W), lambda bi, qi: (bi, qi, 0))
    out_shape = jax.ShapeDtypeStruct((b, lp, ATT_W), BF16)
    return (b, nq), in_specs, out_spec, out_shape


def _sb_kernel(qT_ref, k_ref, vT_ref, o_ref, qm_ref, upper_ref, lb_ref, lr_ref, acc_ref, later_ref):
    qi = pl.program_id(1)
    _store_masked_rows(qm_ref, qT_ref[0, 0], HEAD_DIM)
    @pl.when((pl.program_id(0) == 0) & (qi == 0))
    def _():
        si = lax.broadcasted_iota(jnp.int32, (KB, KB), 0)
        ji = lax.broadcasted_iota(jnp.int32, (KB, KB), 1)
        upper_ref[0:KB, :] = (ji > si).astype(BF16)
        upper_ref[KB:, :] = jnp.ones((SUM_ROWS, KB), BF16)

    acc_ref[...] = jnp.zeros_like(acc_ref)
    later_ref[...] = jnp.zeros_like(later_ref)

    def a_head(h, kb, slot, valid):
        kblk = k_ref[0, pl.ds(pl.multiple_of(kb * KB, KB), KB), :]
        z = _dot(kblk, qm_ref[h])
        if valid is not None:
            z = jnp.where(valid, z, NEG_BIG)
        neg_abs = lax.bitcast_convert_type(lax.bitcast_convert_type(z, jnp.int32) | SIGN_BIT, F32)
        sp = jnp.log(1.0 + jnp.exp2(neg_abs)) * LOG2E
        log_beta = jnp.minimum(z, 0.0) - sp
        lb_ref[slot * N_HEADS + h] = log_beta
        lr_ref[slot * N_HEADS + h] = (log_beta - z).astype(BF16)

    def b_intra(h, slot):
        return _dot(upper_ref[...], lr_ref[slot * N_HEADS + h])

    def b_value(h, kb, slot, sums):
        hs = slice(h * HEAD_DIM, (h + 1) * HEAD_DIM)
        w = jnp.exp2(lb_ref[slot * N_HEADS + h] + sums[0:KB, :])
        later = later_ref[h:h + 1, :]
        acc_ref[hs, :] += _dot(vT_ref[0, kb, hs, :], w.astype(BF16)) * jnp.exp2(later)
        later_ref[h:h + 1, :] = later + sums[KB:KB + 1, :]

    def stage_a(kb, slot, kind):
        valid = _key_mask(kind, qi, kb, inclusive=False)
        for h in range(N_HEADS):
            a_head(h, kb, slot, valid)

    def stage_b(kb, slot):
        intra = [b_intra(h, slot) for h in range(N_HEADS)]
        for h in range(N_HEADS):
            b_value(h, kb, slot, intra[h])

    def stage_ba(kb, slot, kind):
        valid = None if kind is None else _key_mask(kind, qi, kb - 1, inclusive=False)
        intra = {}
        for h in range(N_HEADS + 1):
            if h < N_HEADS:
                intra[h] = b_intra(h, slot)
                a_head(h, kb - 1, 1 - slot, valid)
            if h >= 1:
                b_value(h - 1, kb, slot, intra.pop(h - 1))

    def rest_is_zero():
        return jnp.max(later_ref[0:N_HEADS, :]) < -UNDERFLOW_LOG2

    _causal_sweep(qi, stage_a, stage_b, stage_ba, rest_is_zero)
    o_ref[0] = acc_ref[...].T.astype(o_ref.dtype)


def _sb_attention(q_t, k, v_t):
    b, lp, _ = k.shape
    grid, in_specs, out_spec, out_shape = _att_specs(b, lp, ATT_W)
    return pl.pallas_call(
        _sb_kernel,
        grid=grid,
        in_specs=in_specs,
        out_specs=out_spec,
        out_shape=out_shape,
        scratch_shapes=[pltpu.VMEM((N_HEADS, ATT_W, TQ), BF16), pltpu.VMEM((KB + SUM_ROWS, KB), BF16),
                        pltpu.VMEM((2 * N_HEADS, KB, TQ), F32), pltpu.VMEM((2 * N_HEADS, KB, TQ), BF16),
                        pltpu.VMEM((ATT_W, TQ), F32), pltpu.VMEM((8, TQ), F32)],
        compiler_params=pltpu.CompilerParams(
            dimension_semantics=("arbitrary", "arbitrary"), vmem_limit_bytes=VMEM_LIMIT),
        name="sb_attention",
    )(q_t, k, v_t)


def _diff_kernel(qT_ref, k_ref, vT_ref, lamp_ref, li_ref, g_ref, o_ref,
                 qm_ref, s_ref, bmax_ref, acc_ref, m_ref):
    qi = pl.program_id(1)
    _store_masked_rows(qm_ref, qT_ref[0, 0], DIFF_SUB)
    acc_ref[...] = jnp.zeros_like(acc_ref)
    m_ref[...] = jnp.full_like(m_ref, NEG_BIG)
    n_maps = 2 * N_HEADS

    def a_map(i, kb, slot, valid):
        kblk = k_ref[0, pl.ds(pl.multiple_of(kb * KB, KB), KB), :]
        s = _dot(kblk, qm_ref[i])
        if valid is not None:
            s = jnp.where(valid, s, NEG_BIG)
        s_ref[slot * n_maps + i] = s.astype(BF16)
        bmax_ref[slot, i:i + 1, :] = jnp.max(s, axis=0, keepdims=True)

    def b_map(i, kb, slot):
        vh = vT_ref[0, kb, (i // 2) * V_ROWS:(i // 2 + 1) * V_ROWS, :]
        m_old = m_ref[i:i + 1, :]
        m_new = jnp.maximum(m_old, bmax_ref[slot, i:i + 1, :]).astype(BF16)
        alpha = jnp.exp2(m_old - m_new.astype(F32))
        p = jnp.exp2(s_ref[slot * n_maps + i] - m_new)
        acc_ref[i] = alpha * acc_ref[i] + _dot(vh, p)
        m_ref[i:i + 1, :] = m_new.astype(F32)

    def stage_a(kb, slot, kind):
        valid = _key_mask(kind, qi, kb, inclusive=True)
        for i in range(n_maps):
            a_map(i, kb, slot, valid)

    def stage_b(kb, slot):
        for i in range(n_maps):
            b_map(i, kb, slot)

    def stage_ba(kb, slot, kind):
        valid = None if kind is None else _key_mask(kind, qi, kb - 1, inclusive=True)
        for i in range(n_maps):
            a_map(i, kb - 1, 1 - slot, valid)
            b_map(i, kb, slot)

    _causal_sweep(qi, stage_a, stage_b, stage_ba)

    lp = lamp_ref[...]
    lam_init = li_ref[...]
    lam = (jnp.exp(jnp.sum(lp[0:1] * lp[1:2], axis=1, keepdims=True))
           - jnp.exp(jnp.sum(lp[2:3] * lp[3:4], axis=1, keepdims=True)) + lam_init)
    outs = []
    for h in range(N_HEADS):
        a1 = acc_ref[2 * h]
        a2 = acc_ref[2 * h + 1]
        o = (a1[:HEAD_DIM] / a1[HEAD_DIM:HEAD_DIM + 1]
             - lam * (a2[:HEAD_DIM] / a2[HEAD_DIM:HEAD_DIM + 1]))
        ms = jnp.mean(o * o, axis=0, keepdims=True)
        outs.append(o * lax.rsqrt(ms + LN_EPS) * g_ref[...] * (1.0 - lam_init))
    o_ref[0] = jnp.concatenate(outs, axis=0).T.astype(o_ref.dtype)


def _diff_attention(q_t, k, v_t, lamp, lam_init, g):
    b, lp, _ = k.shape
    grid, in_specs, out_spec, out_shape = _att_specs(b, lp, N_HEADS * V_ROWS)
    in_specs = in_specs + [
        pl.BlockSpec((4, DIFF_SUB), lambda bi, qi: (0, 0)),
        pl.BlockSpec((1, 1), lambda bi, qi: (0, 0)),
        pl.BlockSpec((HEAD_DIM, 1), lambda bi, qi: (0, 0)),
    ]
    return pl.pallas_call(
        _diff_kernel,
        grid=grid,
        in_specs=in_specs,
        out_specs=out_spec,
        out_shape=out_shape,
        scratch_shapes=[pltpu.VMEM((2 * N_HEADS, ATT_W, TQ), BF16), pltpu.VMEM((4 * N_HEADS, KB, TQ), BF16),
                        pltpu.VMEM((2, 8, TQ), F32),
                        pltpu.VMEM((2 * N_HEADS, V_ROWS, TQ), F32), pltpu.VMEM((8, TQ), F32)],
        compiler_params=pltpu.CompilerParams(
            dimension_semantics=("arbitrary", "arbitrary"), vmem_limit_bytes=VMEM_LIMIT),
        name="diff_attention",
    )(q_t, k, v_t, lamp, lam_init, g)


def _layer_norm(x, g, b):
    mu = jnp.mean(x, axis=-1, keepdims=True)
    xc = x - mu
    var = jnp.mean(xc * xc, axis=-1, keepdims=True)
    return xc * lax.rsqrt(var + LN_EPS) * g + b


def _route(logits_t, bias_ref):
    lg = [logits_t[e:e + 1, :] for e in range(N_EXPERTS)]
    mx = functools.reduce(jnp.maximum, lg)
    ex = [jnp.exp(v - mx) for v in lg]
    den = functools.reduce(jnp.add, ex)
    probs = [v / den for v in ex]
    sel = [probs[e] + bias_ref[e:e + 1, :] for e in range(N_EXPERTS)]

    group_score = []
    for g in range(N_GROUPS):
        a = sel[g * EXPERTS_PER_GROUP:(g + 1) * EXPERTS_PER_GROUP]
        pair_sums = [a[i] + a[j] for i in range(EXPERTS_PER_GROUP) for j in range(i + 1, EXPERTS_PER_GROUP)]
        group_score.append(functools.reduce(jnp.maximum, pair_sums))
    best = group_score[0]
    gidx = jnp.zeros(best.shape, jnp.int32)
    for g in range(1, N_GROUPS):
        better = group_score[g] > best
        gidx = jnp.where(better, g, gidx)
        best = jnp.where(better, group_score[g], best)

    def in_group(vals):
        out = []
        for j in range(EXPERTS_PER_GROUP):
            v = vals[(N_GROUPS - 1) * EXPERTS_PER_GROUP + j]
            for g in range(N_GROUPS - 2, -1, -1):
                v = jnp.where(gidx == g, vals[g * EXPERTS_PER_GROUP + j], v)
            out.append(v)
        return out

    s_in = in_group(sel)
    p_in = in_group(probs)
    i1 = jnp.zeros(best.shape, jnp.int32)
    v1 = s_in[0]
    for j in range(1, EXPERTS_PER_GROUP):
        better = s_in[j] > v1
        i1 = jnp.where(better, j, i1)
        v1 = jnp.where(better, s_in[j], v1)
    i2 = jnp.full(best.shape, -1, jnp.int32)
    v2 = jnp.full(best.shape, -jnp.inf, F32)
    for j in range(EXPERTS_PER_GROUP):
        better = (i1 != j) & (s_in[j] > v2)
        i2 = jnp.where(better, j, i2)
        v2 = jnp.where(better, s_in[j], v2)
    w1 = functools.reduce(jnp.add, [jnp.where(i1 == j, p_in[j], 0.0) for j in range(EXPERTS_PER_GROUP)])
    w2 = functools.reduce(jnp.add, [jnp.where(i2 == j, p_in[j], 0.0) for j in range(EXPERTS_PER_GROUP)])
    wsum = w1 + w2
    w1 = w1 / wsum
    w2 = w2 / wsum
    gates = []
    for e in range(N_EXPERTS):
        g, j = divmod(e, EXPERTS_PER_GROUP)
        in_g = gidx == g
        gates.append(jnp.where(in_g & (i1 == j), w1, 0.0) + jnp.where(in_g & (i2 == j), w2, 0.0))
    return gates


def _post_kernel(ysb_ref, ydf_ref, ycv_ref, h_ref, wo_ref, g1_ref, b1_ref, rw_ref, rb_ref,
                 wg_ref, wu_ref, wd_ref, g2_ref, b2_ref, o_ref, gate_ref):
    @pl.when(pl.program_id(0) == 0)
    def _():
        gate_ref[...] = jnp.zeros_like(gate_ref)

    pm = h_ref.shape[0] // POST_PARTS
    parts = [slice(i * pm, (i + 1) * pm) for i in range(POST_PARTS)]
    n_chunks = FF_W // FF_CHUNK

    def out_proj(r):
        return (_dot(ysb_ref[r, :], wo_ref[0:ATT_W, :])
                + _dot(ydf_ref[r, :], wo_ref[ATT_W:2 * ATT_W, :])
                + _dot(ycv_ref[r, :], wo_ref[2 * ATT_W:, :]))

    def norm_route(i, mix):
        h1 = _layer_norm(ALPHA * h_ref[parts[i], :] + mix, g1_ref[...], b1_ref[...])
        x_hi = h1.astype(BF16)
        x_lo = (h1 - x_hi.astype(F32)).astype(BF16)
        logits_t = _dot_nt(rw_ref[0], x_hi) + _dot_nt(rw_ref[1], x_hi) + _dot_nt(rw_ref[0], x_lo)
        gates = _route(logits_t, rb_ref)
        for e in range(N_EXPERTS):
            gate_ref[i, e:e + 1, :] = gates[e]
        return h1, x_hi, gate_ref[i].T

    def expert_chunk(c, x_hi, gate_cols):
        cs = slice(c * FF_CHUNK, (c + 1) * FF_CHUNK)
        gate_act = _dot(x_hi, wg_ref[:, cs])
        up = _dot(x_hi, wu_ref[:, cs])
        hid = gate_act * (1.0 / (1.0 + jnp.exp(-gate_act))) * up
        parts = []
        for k in range(FF_CHUNK // D_FF_EXPERT):
            e = c * (FF_CHUNK // D_FF_EXPERT) + k
            parts.append(hid[:, k * D_FF_EXPERT:(k + 1) * D_FF_EXPERT] * gate_cols[:, e:e + 1])
        return _dot(jnp.concatenate(parts, axis=1).astype(BF16), wd_ref[cs, :])

    mix = [out_proj(r) for r in parts]
    routed = [norm_route(i, mix[i]) for i in range(POST_PARTS)]
    for i in range(POST_PARTS):
        h1, x_hi, gate_cols = routed[i]
        ffn = expert_chunk(0, x_hi, gate_cols)
        for c in range(1, n_chunks):
            ffn = ffn + expert_chunk(c, x_hi, gate_cols)
        o_ref[parts[i], :] = _layer_norm(ALPHA * h1 + ffn, g2_ref[...], b2_ref[...])


def _post(ysb, ydf, ycv, h, wo, g1, b1, rw, rb, wg, wu, wd, g2, b2):
    t = h.shape[0]
    tm = 1024 if t % 1024 == 0 else 512
    row = lambda w: pl.BlockSpec((tm, w), lambda i: (i, 0))
    full = lambda shape: pl.BlockSpec(shape, lambda i: (0,) * len(shape), pipeline_mode=pl.Buffered(1))
    return pl.pallas_call(
        _post_kernel,
        grid=(t // tm,),
        in_specs=[row(ATT_W), row(ATT_W), row(CONV_W), row(D_MODEL),
                  full((D_MODEL, D_MODEL)), full((1, D_MODEL)), full((1, D_MODEL)),
                  full((2, N_EXPERTS, D_MODEL)), full((N_EXPERTS, 1)),
                  full((D_MODEL, FF_W)), full((D_MODEL, FF_W)), full((FF_W, D_MODEL)),
                  full((1, D_MODEL)), full((1, D_MODEL))],
        out_specs=row(D_MODEL),
        out_shape=jax.ShapeDtypeStruct((t, D_MODEL), F32),
        scratch_shapes=[pltpu.VMEM((POST_PARTS, 128, tm // POST_PARTS), F32)],
        compiler_params=pltpu.CompilerParams(
            dimension_semantics=("arbitrary",), vmem_limit_bytes=VMEM_LIMIT),
        name="out_proj_moe",
    )(ysb, ydf, ycv, h, wo, g1, b1, rw, rb, wg, wu, wd, g2, b2)


def _rope_tables(lp):
    half = ROPE_DIM // 2
    pos = (jnp.arange(lp) - PAD_FRONT).astype(F32)
    inv = jnp.power(jnp.float32(ROPE_THETA), -jnp.arange(0, ROPE_DIM, 2, dtype=F32) / ROPE_DIM)
    ang = pos[:, None] * inv[None, :]
    cos, sin = jnp.cos(ang), jnp.sin(ang)
    pad = jnp.zeros((lp, DIFF_SUB - ROPE_DIM), F32)
    zero = jnp.zeros((lp, half), F32)
    rc = jnp.concatenate([cos, cos, pad + 1.0], axis=1)
    rm = jnp.concatenate([-sin, zero, pad], axis=1)
    rp = jnp.concatenate([zero, sin, pad], axis=1)
    reps = ATT_W // DIFF_SUB
    return tuple(jnp.tile(t, (1, reps)) for t in (rc, rm, rp))


def kernel(x, meta_tokens, w_in, conv_w, lambda_q1, lambda_k1, lambda_q2, lambda_k2, diff_norm_g, w_out,
           ln1_g, ln1_b, router_w, router_bias, w_gate, w_up, w_down, ln2_g, ln2_b):
    b, seq, _ = x.shape
    length = BLOCK + seq
    lp = -(-length // TQ) * TQ
    lead = jnp.concatenate([jnp.zeros((PAD_FRONT, D_MODEL), x.dtype), meta_tokens.astype(x.dtype)], axis=0)
    h = jnp.concatenate([jnp.broadcast_to(lead[None], (b, BLOCK, D_MODEL)), x,
                         jnp.zeros((b, lp - length, D_MODEL), x.dtype)], axis=1)
    rc, rm, rp = _rope_tables(lp)

    rw_t = router_w.T.astype(F32)
    rw_hi = rw_t.astype(BF16)
    rw_lo = (rw_t - rw_hi.astype(F32)).astype(BF16)
    rw = jnp.stack([rw_hi, rw_lo])
    rb = router_bias.astype(F32).reshape(N_EXPERTS, 1)

    for l in range(DEPTH):
        lam_init = jnp.full((1, 1), 0.8 - 0.6 * math.exp(-0.3 * l), F32)
        lamp = jnp.stack([lambda_q1[l], lambda_k1[l], lambda_q2[l], lambda_k2[l]]).astype(F32)
        sq_t, sk, sv_t, dq_t, dk, dv_t, y_conv = _in_proj(
            h, w_in[l].astype(BF16), rc, rm, rp, conv_w[l].astype(F32))
        y_sb = _sb_attention(sq_t, sk, sv_t)
        y_diff = _diff_attention(dq_t, dk, dv_t, lamp, lam_init, diff_norm_g[l].astype(F32).reshape(HEAD_DIM, 1))
        wg = w_gate[l].transpose(1, 0, 2).reshape(D_MODEL, FF_W).astype(BF16)
        wu = w_up[l].transpose(1, 0, 2).reshape(D_MODEL, FF_W).astype(BF16)
        wd = w_down[l].reshape(FF_W, D_MODEL).astype(BF16)
        flat = lambda a: a.reshape(b * lp, a.shape[-1])
        h = _post(flat(y_sb), flat(y_diff), flat(y_conv), flat(h), w_out[l].astype(BF16),
                  ln1_g[l].reshape(1, D_MODEL), ln1_b[l].reshape(1, D_MODEL), rw, rb, wg, wu, wd,
                  ln2_g[l].reshape(1, D_MODEL), ln2_b[l].reshape(1, D_MODEL)).reshape(b, lp, D_MODEL)
    return h[:, BLOCK:length]
```

```python
import functools
import math

import jax
import jax.numpy as jnp
from jax import lax
from jax.experimental import pallas as pl
from jax.experimental.pallas import tpu as pltpu

D_MODEL = 1024
DEPTH = 4
N_META = 16
BLOCK = 128
PAD_FRONT = BLOCK - N_META
HEAD_DIM = 64
N_HEADS = 4
ATT_W = N_HEADS * HEAD_DIM
CONV_W = 512
IN_COLS = 6 * ATT_W + 3 * CONV_W
DIFF_SUB = HEAD_DIM // 2
ROPE_DIM = DIFF_SUB // 4
ROPE_THETA = 500000.0
CONV_K = 3
N_EXPERTS = 16
N_GROUPS = 4
EXPERTS_PER_GROUP = N_EXPERTS // N_GROUPS
D_FF_EXPERT = 128
FF_W = N_EXPERTS * D_FF_EXPERT
FF_CHUNK = 512
POST_PARTS = 2
ALPHA = (2.0 * DEPTH) ** 0.25
LN_EPS = 1e-5
NEG_BIG = -(2.0 ** 100)

TQ = 256
KB = 256
CONV_HALO = 8
SWEEP_UNROLL = 4

SUM_ROWS = 16
V_ROWS = HEAD_DIM + SUM_ROWS
LOG2E = math.log2(math.e)
SIGN_BIT = -2 ** 31
UNDERFLOW_LOG2 = 151.0

F32 = jnp.float32
BF16 = jnp.bfloat16
VMEM_LIMIT = 56 * 1024 * 1024


def _dot(a, b):
    return jnp.dot(a, b, preferred_element_type=F32)


def _dot_nt(a, b):
    return lax.dot_general(a, b, (((1,), (1,)), ((), ())), preferred_element_type=F32)


def _in_proj_kernel(h_ref, w_ref, rc_ref, rm_ref, rp_ref, cw_ref,
                    sqT_ref, sk_ref, svT_ref, dqT_ref, dk_ref, dvT_ref, yc_ref, ubuf_ref, *, tm):
    j = pl.program_id(1)
    x = h_ref[0].astype(BF16)

    def proj(c0, width):
        return _dot(x, w_ref[:, c0:c0 + width])

    def store_t(o_ref, val):
        for s in range(tm // TQ):
            o_ref[0, s] = val[s * TQ:(s + 1) * TQ, :].T.astype(o_ref.dtype)

    def rope(t):
        halves = []
        for half in range(ATT_W // 128):
            sl = slice(half * 128, (half + 1) * 128)
            th = t[:, sl]
            halves.append(th * rc_ref[:, sl]
                          + pltpu.roll(th, 128 - ROPE_DIM // 2, axis=1) * rm_ref[:, sl]
                          + pltpu.roll(th, ROPE_DIM // 2, axis=1) * rp_ref[:, sl])
        return jnp.concatenate(halves, axis=1)

    store_t(sqT_ref, proj(0, ATT_W) * (HEAD_DIM ** -0.5 * LOG2E))
    sk_ref[0] = proj(ATT_W, ATT_W).astype(BF16)
    store_t(svT_ref, proj(2 * ATT_W, ATT_W))
    store_t(dqT_ref, rope(proj(3 * ATT_W, ATT_W)) * (DIFF_SUB ** -0.5 * LOG2E))
    dk_ref[0] = rope(proj(4 * ATT_W, ATT_W)).astype(BF16)
    dv = proj(5 * ATT_W, ATT_W)
    ones = jnp.ones((V_ROWS - HEAD_DIM, TQ), BF16)
    for s in range(tm // TQ):
        dv_t = dv[s * TQ:(s + 1) * TQ, :].T.astype(BF16)
        for h in range(N_HEADS):
            dvT_ref[0, s, h * V_ROWS:h * V_ROWS + HEAD_DIM, :] = dv_t[h * HEAD_DIM:(h + 1) * HEAD_DIM, :]
            dvT_ref[0, s, h * V_ROWS + HEAD_DIM:(h + 1) * V_ROWS, :] = ones

    c0 = 6 * ATT_W
    cb = proj(c0, CONV_W)
    cc = proj(c0 + CONV_W, CONV_W)
    ch = proj(c0 + 2 * CONV_W, CONV_W)
    pos = j * tm + lax.broadcasted_iota(jnp.int32, (tm, CONV_W), 0)
    u = jnp.where(pos >= PAD_FRONT, cc * ch, 0.0)

    @pl.when(j == 0)
    def _():
        ubuf_ref[0:CONV_HALO, :] = jnp.zeros((CONV_HALO, CONV_W), F32)

    ubuf_ref[CONV_HALO:CONV_HALO + tm, :] = u
    u1 = ubuf_ref[CONV_HALO - 1:CONV_HALO - 1 + tm, :]
    u2 = ubuf_ref[CONV_HALO - 2:CONV_HALO - 2 + tm, :]
    conv = cw_ref[0:1, :] * u2 + cw_ref[1:2, :] * u1 + cw_ref[2:3, :] * u
    yc_ref[0] = (cb * conv).astype(BF16)
    ubuf_ref[0:CONV_HALO, :] = ubuf_ref[tm:tm + CONV_HALO, :]


def _in_proj(h, w, rc, rm, rp, cw):
    b, lp, _ = h.shape
    tm = 768 if lp % 768 == 0 else TQ
    nj = lp // tm
    nq = lp // TQ
    t_shape = jax.ShapeDtypeStruct((b, nq, ATT_W, TQ), BF16)
    n_shape = jax.ShapeDtypeStruct((b, lp, ATT_W), BF16)
    v_shape = jax.ShapeDtypeStruct((b, nq, N_HEADS * V_ROWS, TQ), BF16)
    t_spec = pl.BlockSpec((1, tm // TQ, ATT_W, TQ), lambda bi, j: (bi, j, 0, 0))
    v_spec = pl.BlockSpec((1, tm // TQ, N_HEADS * V_ROWS, TQ), lambda bi, j: (bi, j, 0, 0))
    n_spec = pl.BlockSpec((1, tm, ATT_W), lambda bi, j: (bi, j, 0))
    tab_spec = pl.BlockSpec((tm, ATT_W), lambda bi, j: (j, 0))
    return pl.pallas_call(
        functools.partial(_in_proj_kernel, tm=tm),
        grid=(b, nj),
        in_specs=[
            pl.BlockSpec((1, tm, D_MODEL), lambda bi, j: (bi, j, 0)),
            pl.BlockSpec((D_MODEL, IN_COLS), lambda bi, j: (0, 0)),
            tab_spec, tab_spec, tab_spec,
            pl.BlockSpec((CONV_K, CONV_W), lambda bi, j: (0, 0)),
        ],
        out_specs=[t_spec, n_spec, t_spec, t_spec, n_spec, v_spec,
                   pl.BlockSpec((1, tm, CONV_W), lambda bi, j: (bi, j, 0))],
        out_shape=[t_shape, n_shape, t_shape, t_shape, n_shape, v_shape,
                   jax.ShapeDtypeStruct((b, lp, CONV_W), BF16)],
        scratch_shapes=[pltpu.VMEM((tm + CONV_HALO, CONV_W), F32)],
        compiler_params=pltpu.CompilerParams(
            dimension_semantics=("arbitrary", "arbitrary"), vmem_limit_bytes=VMEM_LIMIT),
        name="in_proj",
    )(h, w, rc, rm, rp, cw)


def _store_masked_rows(qm_ref, q_t, width):
    frow = lax.broadcasted_iota(jnp.int32, q_t.shape, 0)
    zero = jnp.zeros_like(q_t)
    for i in range(ATT_W // width):
        qm_ref[i] = jnp.where((frow >= i * width) & (frow < (i + 1) * width), q_t, zero)


def _causal_sweep(qi, stage_a, stage_b, stage_ba, rest_is_zero=None):
    if rest_is_zero is None:
        _sweep_from(qi, "diag", stage_a, stage_b, stage_ba, None)
        return

    odd = qi & 1
    stage_a(qi, odd, "diag")

    @pl.when(qi == 0)
    def _():
        stage_b(0, 0)

    for parity in (0, 1):
        @pl.when((qi >= 1) & (odd == parity))
        def _(parity=parity):
            stage_ba(qi, parity, "first")

    @pl.when(qi >= 1)
    def _():
        stage_b(qi - 1, 1 - odd)

        @pl.when((qi >= 2) & jnp.logical_not(rest_is_zero()))
        def _():
            _sweep_from(qi - 2, "first", stage_a, stage_b, stage_ba, rest_is_zero)


def _sweep_from(top, top_kind, stage_a, stage_b, stage_ba, rest_is_zero):
    qi = top
    odd = qi & 1
    stage_a(qi, odd, top_kind)

    @pl.when((odd == 0) & (qi >= 2))
    def _():
        stage_ba(qi, 0, None)

    kb_odd = qi - 1 + odd
    n_pairs = jnp.maximum(kb_odd - 1, 0) // 2

    if rest_is_zero is None:
        def pair(kb):
            stage_ba(kb, 1, None)
            stage_ba(kb - 1, 0, None)

        def body(j, carry):
            for u in range(SWEEP_UNROLL):
                pair(kb_odd - 2 * (SWEEP_UNROLL * j + u))
            return carry

        n_trips = n_pairs // SWEEP_UNROLL
        lax.fori_loop(0, n_trips, body, 0)

        def tail(p, carry):
            pair(kb_odd - 2 * p)
            return carry

        lax.fori_loop(SWEEP_UNROLL * n_trips, n_pairs, tail, 0)

        @pl.when(qi >= 1)
        def _():
            stage_ba(1, 1, "first")

        stage_b(0, 0)
    else:
        def cond(carry):
            j, stop = carry
            return (j < n_pairs) & jnp.logical_not(stop)

        def body(carry):
            j, _ = carry
            kb = kb_odd - 2 * j
            stage_ba(kb, 1, None)

            @pl.when(jnp.logical_not(rest_is_zero()))
            def _():
                stage_ba(kb - 1, 0, None)

            return j + 1, rest_is_zero()

        _, stop = lax.while_loop(cond, body, (jnp.int32(0), jnp.zeros((), jnp.bool_)))

        @pl.when(jnp.logical_not(stop))
        def _():
            @pl.when(qi >= 1)
            def _():
                stage_ba(1, 1, "first")

            stage_b(0, 0)


def _key_mask(kind, qi, kb, inclusive):
    key_idx = kb * KB + lax.broadcasted_iota(jnp.int32, (KB, TQ), 0)
    valid = key_idx >= PAD_FRONT
    if kind == "diag":
        q_idx = qi * TQ + lax.broadcasted_iota(jnp.int32, (KB, TQ), 1)
        valid = valid & ((key_idx <= q_idx) if inclusive else (key_idx < q_idx))
    return valid


def _att_specs(b, lp, v_rows):
    nq = lp // TQ
    nkb = lp // KB
    in_specs = [
        pl.BlockSpec((1, 1, ATT_W, TQ), lambda bi, qi: (bi, qi, 0, 0)),
        pl.BlockSpec((1, lp, ATT_W), lambda bi, qi: (bi, 0, 0)),
        pl.BlockSpec((1, nkb, v_rows, KB), lambda bi, qi: (bi, 0, 0, 0)),
    ]
    out_spec = pl.BlockSpec((1, TQ, ATT_W), lambda bi, qi: (bi, qi, 0))
    out_shape = jax.ShapeDtypeStruct((b, lp, ATT_W), BF16)
    return (b, nq), in_specs, out_spec, out_shape


def _sb_kernel(qT_ref, k_ref, vT_ref, o_ref, qm_ref, upper_ref, lb_ref, lr_ref, acc_ref, later_ref):
    qi = pl.program_id(1)
    _store_masked_rows(qm_ref, qT_ref[0, 0], HEAD_DIM)
    @pl.when((pl.program_id(0) == 0) & (qi == 0))
    def _():
        si = lax.broadcasted_iota(jnp.int32, (KB, KB), 0)
        ji = lax.broadcasted_iota(jnp.int32, (KB, KB), 1)
        upper_ref[0:KB, :] = (ji > si).astype(BF16)
        upper_ref[KB:, :] = jnp.ones((SUM_ROWS, KB), BF16)

    acc_ref[...] = jnp.zeros_like(acc_ref)
    later_ref[...] = jnp.zeros_like(later_ref)

    def a_head(h, kb, slot, valid):
        kblk = k_ref[0, pl.ds(pl.multiple_of(kb * KB, KB), KB), :]
        z = _dot(kblk, qm_ref[h])
        if valid is not None:
            z = jnp.where(valid, z, NEG_BIG)
        neg_abs = lax.bitcast_convert_type(lax.bitcast_convert_type(z, jnp.int32) | SIGN_BIT, F32)
        sp = jnp.log(1.0 + jnp.exp2(neg_abs)) * LOG2E
        log_beta = jnp.minimum(z, 0.0) - sp
        lb_ref[slot * N_HEADS + h] = log_beta
        lr_ref[slot * N_HEADS + h] = (log_beta - z).astype(BF16)

    def b_intra(h, slot):
        return _dot(upper_ref[...], lr_ref[slot * N_HEADS + h])

    def b_value(h, kb, slot, sums):
        hs = slice(h * HEAD_DIM, (h + 1) * HEAD_DIM)
        w = jnp.exp2(lb_ref[slot * N_HEADS + h] + sums[0:KB, :])
        later = later_ref[h:h + 1, :]
        acc_ref[hs, :] += _dot(vT_ref[0, kb, hs, :], w.astype(BF16)) * jnp.exp2(later)
        later_ref[h:h + 1, :] = later + sums[KB:KB + 1, :]

    def stage_a(kb, slot, kind):
        valid = _key_mask(kind, qi, kb, inclusive=False)
        for h in range(N_HEADS):
            a_head(h, kb, slot, valid)

    def stage_b(kb, slot):
        intra = [b_intra(h, slot) for h in range(N_HEADS)]
        for h in range(N_HEADS):
            b_value(h, kb, slot, intra[h])

    def stage_ba(kb, slot, kind):
        valid = None if kind is None else _key_mask(kind, qi, kb - 1, inclusive=False)
        intra = {}
        for h in range(N_HEADS + 1):
            if h < N_HEADS:
                intra[h] = b_intra(h, slot)
                a_head(h, kb - 1, 1 - slot, valid)
            if h >= 1:
                b_value(h - 1, kb, slot, intra.pop(h - 1))

    def rest_is_zero():
        return jnp.max(later_ref[0:N_HEADS, :]) < -UNDERFLOW_LOG2

    _causal_sweep(qi, stage_a, stage_b, stage_ba, rest_is_zero)
    o_ref[0] = acc_ref[...].T.astype(o_ref.dtype)


def _sb_attention(q_t, k, v_t):
    b, lp, _ = k.shape
    grid, in_specs, out_spec, out_shape = _att_specs(b, lp, ATT_W)
    return pl.pallas_call(
        _sb_kernel,
        grid=grid,
        in_specs=in_specs,
        out_specs=out_spec,
        out_shape=out_shape,
        scratch_shapes=[pltpu.VMEM((N_HEADS, ATT_W, TQ), BF16), pltpu.VMEM((KB + SUM_ROWS, KB), BF16),
                        pltpu.VMEM((2 * N_HEADS, KB, TQ), F32), pltpu.VMEM((2 * N_HEADS, KB, TQ), BF16),
                        pltpu.VMEM((ATT_W, TQ), F32), pltpu.VMEM((8, TQ), F32)],
        compiler_params=pltpu.CompilerParams(
            dimension_semantics=("arbitrary", "arbitrary"), vmem_limit_bytes=VMEM_LIMIT),
        name="sb_attention",
    )(q_t, k, v_t)


def _diff_kernel(qT_ref, k_ref, vT_ref, lamp_ref, li_ref, g_ref, o_ref,
                 qm_ref, s_ref, bmax_ref, acc_ref, m_ref):
    qi = pl.program_id(1)
    _store_masked_rows(qm_ref, qT_ref[0, 0], DIFF_SUB)
    acc_ref[...] = jnp.zeros_like(acc_ref)
    m_ref[...] = jnp.full_like(m_ref, NEG_BIG)
    n_maps = 2 * N_HEADS

    def a_map(i, kb, slot, valid):
        kblk = k_ref[0, pl.ds(pl.multiple_of(kb * KB, KB), KB), :]
        s = _dot(kblk, qm_ref[i])
        if valid is not None:
            s = jnp.where(valid, s, NEG_BIG)
        s_ref[slot * n_maps + i] = s.astype(BF16)
        bmax_ref[slot, i:i + 1, :] = jnp.max(s, axis=0, keepdims=True)

    def b_map(i, kb, slot):
        vh = vT_ref[0, kb, (i // 2) * V_ROWS:(i // 2 + 1) * V_ROWS, :]
        m_old = m_ref[i:i + 1, :]
        m_new = jnp.maximum(m_old, bmax_ref[slot, i:i + 1, :]).astype(BF16)
        alpha = jnp.exp2(m_old - m_new.astype(F32))
        p = jnp.exp2(s_ref[slot * n_maps + i] - m_new)
        acc_ref[i] = alpha * acc_ref[i] + _dot(vh, p)
        m_ref[i:i + 1, :] = m_new.astype(F32)

    def stage_a(kb, slot, kind):
        valid = _key_mask(kind, qi, kb, inclusive=True)
        for i in range(n_maps):
            a_map(i, kb, slot, valid)

    def stage_b(kb, slot):
        for i in range(n_maps):
            b_map(i, kb, slot)

    def stage_ba(kb, slot, kind):
        valid = None if kind is None else _key_mask(kind, qi, kb - 1, inclusive=True)
        for i in range(n_maps):
            a_map(i, kb - 1, 1 - slot, valid)
            b_map(i, kb, slot)

    _causal_sweep(qi, stage_a, stage_b, stage_ba)

    lp = lamp_ref[...]
    lam_init = li_ref[...]
    lam = (jnp.exp(jnp.sum(lp[0:1] * lp[1:2], axis=1, keepdims=True))
           - jnp.exp(jnp.sum(lp[2:3] * lp[3:4], axis=1, keepdims=True)) + lam_init)
    outs = []
    for h in range(N_HEADS):
        a1 = acc_ref[2 * h]
        a2 = acc_ref[2 * h + 1]
        o = (a1[:HEAD_DIM] / a1[HEAD_DIM:HEAD_DIM + 1]
             - lam * (a2[:HEAD_DIM] / a2[HEAD_DIM:HEAD_DIM + 1]))
        ms = jnp.mean(o * o, axis=0, keepdims=True)
        outs.append(o * lax.rsqrt(ms + LN_EPS) * g_ref[...] * (1.0 - lam_init))
    o_ref[0] = jnp.concatenate(outs, axis=0).T.astype(o_ref.dtype)


def _diff_attention(q_t, k, v_t, lamp, lam_init, g):
    b, lp, _ = k.shape
    grid, in_specs, out_spec, out_shape = _att_specs(b, lp, N_HEADS * V_ROWS)
    in_specs = in_specs + [
        pl.BlockSpec((4, DIFF_SUB), lambda bi, qi: (0, 0)),
        pl.BlockSpec((1, 1), lambda bi, qi: (0, 0)),
        pl.BlockSpec((HEAD_DIM, 1), lambda bi, qi: (0, 0)),
    ]
    return pl.pallas_call(
        _diff_kernel,
        grid=grid,
        in_specs=in_specs,
        out_specs=out_spec,
        out_shape=out_shape,
        scratch_shapes=[pltpu.VMEM((2 * N_HEADS, ATT_W, TQ), BF16), pltpu.VMEM((4 * N_HEADS, KB, TQ), BF16),
                        pltpu.VMEM((2, 8, TQ), F32),
                        pltpu.VMEM((2 * N_HEADS, V_ROWS, TQ), F32), pltpu.VMEM((8, TQ), F32)],
        compiler_params=pltpu.CompilerParams(
            dimension_semantics=("arbitrary", "arbitrary"), vmem_limit_bytes=VMEM_LIMIT),
        name="diff_attention",
    )(q_t, k, v_t, lamp, lam_init, g)


def _layer_norm(x, g, b):
    mu = jnp.mean(x, axis=-1, keepdims=True)
    xc = x - mu
    var = jnp.mean(xc * xc, axis=-1, keepdims=True)
    return xc * lax.rsqrt(var + LN_EPS) * g + b


def _route(logits_t, bias_ref):
    lg = [logits_t[e:e + 1, :] for e in range(N_EXPERTS)]
    mx = functools.reduce(jnp.maximum, lg)
    ex = [jnp.exp(v - mx) for v in lg]
    den = functools.reduce(jnp.add, ex)
    probs = [v / den for v in ex]
    sel = [probs[e] + bias_ref[e:e + 1, :] for e in range(N_EXPERTS)]

    group_score = []
    for g in range(N_GROUPS):
        a = sel[g * EXPERTS_PER_GROUP:(g + 1) * EXPERTS_PER_GROUP]
        pair_sums = [a[i] + a[j] for i in range(EXPERTS_PER_GROUP) for j in range(i + 1, EXPERTS_PER_GROUP)]
        group_score.append(functools.reduce(jnp.maximum, pair_sums))
    best = group_score[0]
    gidx = jnp.zeros(best.shape, jnp.int32)
    for g in range(1, N_GROUPS):
        better = group_score[g] > best
        gidx = jnp.where(better, g, gidx)
        best = jnp.where(better, group_score[g], best)

    def in_group(vals):
        out = []
        for j in range(EXPERTS_PER_GROUP):
            v = vals[(N_GROUPS - 1) * EXPERTS_PER_GROUP + j]
            for g in range(N_GROUPS - 2, -1, -1):
                v = jnp.where(gidx == g, vals[g * EXPERTS_PER_GROUP + j], v)
            out.append(v)
        return out

    s_in = in_group(sel)
    p_in = in_group(probs)
    i1 = jnp.zeros(best.shape, jnp.int32)
    v1 = s_in[0]
    for j in range(1, EXPERTS_PER_GROUP):
        better = s_in[j] > v1
        i1 = jnp.where(better, j, i1)
        v1 = jnp.where(better, s_in[j], v1)
    i2 = jnp.full(best.shape, -1, jnp.int32)
    v2 = jnp.full(best.shape, -jnp.inf, F32)
    for j in range(EXPERTS_PER_GROUP):
        better = (i1 != j) & (s_in[j] > v2)
        i2 = jnp.where(better, j, i2)
        v2 = jnp.where(better, s_in[j], v2)
    w1 = functools.reduce(jnp.add, [jnp.where(i1 == j, p_in[j], 0.0) for j in range(EXPERTS_PER_GROUP)])
    w2 = functools.reduce(jnp.add, [jnp.where(i2 == j, p_in[j], 0.0) for j in range(EXPERTS_PER_GROUP)])
    wsum = w1 + w2
    w1 = w1 / wsum
    w2 = w2 / wsum
    gates = []
    for e in range(N_EXPERTS):
        g, j = divmod(e, EXPERTS_PER_GROUP)
        in_g = gidx == g
        gates.append(jnp.where(in_g & (i1 == j), w1, 0.0) + jnp.where(in_g & (i2 == j), w2, 0.0))
    return gates


def _post_kernel(ysb_ref, ydf_ref, ycv_ref, h_ref, wo_ref, g1_ref, b1_ref, rw_ref, rb_ref,
                 wg_ref, wu_ref, wd_ref, g2_ref, b2_ref, o_ref, gate_ref):
    @pl.when(pl.program_id(0) == 0)
    def _():
        gate_ref[...] = jnp.zeros_like(gate_ref)

    pm = h_ref.shape[0] // POST_PARTS
    parts = [slice(i * pm, (i + 1) * pm) for i in range(POST_PARTS)]
    n_chunks = FF_W // FF_CHUNK

    def out_proj(r):
        return (_dot(ysb_ref[r, :], wo_ref[0:ATT_W, :])
                + _dot(ydf_ref[r, :], wo_ref[ATT_W:2 * ATT_W, :])
                + _dot(ycv_ref[r, :], wo_ref[2 * ATT_W:, :]))

    def norm_route(i, mix):
        h1 = _layer_norm(ALPHA * h_ref[parts[i], :] + mix, g1_ref[...], b1_ref[...])
        x_hi = h1.astype(BF16)
        x_lo = (h1 - x_hi.astype(F32)).astype(BF16)
        logits_t = _dot_nt(rw_ref[0], x_hi) + _dot_nt(rw_ref[1], x_hi) + _dot_nt(rw_ref[0], x_lo)
        gates = _route(logits_t, rb_ref)
        for e in range(N_EXPERTS):
            gate_ref[i, e:e + 1, :] = gates[e]
        return h1, x_hi, gate_ref[i].T

    def expert_chunk(c, x_hi, gate_cols):
        cs = slice(c * FF_CHUNK, (c + 1) * FF_CHUNK)
        gate_act = _dot(x_hi, wg_ref[:, cs])
        up = _dot(x_hi, wu_ref[:, cs])
        hid = gate_act * (1.0 / (1.0 + jnp.exp(-gate_act))) * up
        parts = []
        for k in range(FF_CHUNK // D_FF_EXPERT):
            e = c * (FF_CHUNK // D_FF_EXPERT) + k
            parts.append(hid[:, k * D_FF_EXPERT:(k + 1) * D_FF_EXPERT] * gate_cols[:, e:e + 1])
        return _dot(jnp.concatenate(parts, axis=1).astype(BF16), wd_ref[cs, :])

    mix = [out_proj(r) for r in parts]
    routed = [norm_route(i, mix[i]) for i in range(POST_PARTS)]
    for i in range(POST_PARTS):
        h1, x_hi, gate_cols = routed[i]
        ffn = expert_chunk(0, x_hi, gate_cols)
        for c in range(1, n_chunks):
            ffn = ffn + expert_chunk(c, x_hi, gate_cols)
        o_ref[parts[i], :] = _layer_norm(ALPHA * h1 + ffn, g2_ref[...], b2_ref[...])


def _post(ysb, ydf, ycv, h, wo, g1, b1, rw, rb, wg, wu, wd, g2, b2):
    t = h.shape[0]
    tm = 1024 if t % 1024 == 0 else 512
    assert t % tm == 0, (t, tm)
    row = lambda w: pl.BlockSpec((tm, w), lambda i: (i, 0))
    full = lambda shape: pl.BlockSpec(shape, lambda i: (0,) * len(shape), pipeline_mode=pl.Buffered(1))
    return pl.pallas_call(
        _post_kernel,
        grid=(t // tm,),
        in_specs=[row(ATT_W), row(ATT_W), row(CONV_W), row(D_MODEL),
                  full((D_MODEL, D_MODEL)), full((1, D_MODEL)), full((1, D_MODEL)),
                  full((2, N_EXPERTS, D_MODEL)), full((N_EXPERTS, 1)),
                  full((D_MODEL, FF_W)), full((D_MODEL, FF_W)), full((FF_W, D_MODEL)),
                  full((1, D_MODEL)), full((1, D_MODEL))],
        out_specs=row(D_MODEL),
        out_shape=jax.ShapeDtypeStruct((t, D_MODEL), F32),
        scratch_shapes=[pltpu.VMEM((POST_PARTS, 128, tm // POST_PARTS), F32)],
        compiler_params=pltpu.CompilerParams(
            dimension_semantics=("arbitrary",), vmem_limit_bytes=VMEM_LIMIT),
        name="out_proj_moe",
    )(ysb, ydf, ycv, h, wo, g1, b1, rw, rb, wg, wu, wd, g2, b2)


def _rope_tables(lp):
    half = ROPE_DIM // 2
    pos = (jnp.arange(lp) - PAD_FRONT).astype(F32)
    inv = jnp.power(jnp.float32(ROPE_THETA), -jnp.arange(0, ROPE_DIM, 2, dtype=F32) / ROPE_DIM)
    ang = pos[:, None] * inv[None, :]
    cos, sin = jnp.cos(ang), jnp.sin(ang)
    pad = jnp.zeros((lp, DIFF_SUB - ROPE_DIM), F32)
    zero = jnp.zeros((lp, half), F32)
    rc = jnp.concatenate([cos, cos, pad + 1.0], axis=1)
    rm = jnp.concatenate([-sin, zero, pad], axis=1)
    rp = jnp.concatenate([zero, sin, pad], axis=1)
    reps = ATT_W // DIFF_SUB
    return tuple(jnp.tile(t, (1, reps)) for t in (rc, rm, rp))


def kernel(x, meta_tokens, w_in, conv_w, lambda_q1, lambda_k1, lambda_q2, lambda_k2, diff_norm_g, w_out,
           ln1_g, ln1_b, router_w, router_bias, w_gate, w_up, w_down, ln2_g, ln2_b):
    b, seq, _ = x.shape
    length = BLOCK + seq
    lp = -(-length // TQ) * TQ
    lead = jnp.concatenate([jnp.zeros((PAD_FRONT, D_MODEL), x.dtype), meta_tokens.astype(x.dtype)], axis=0)
    h = jnp.concatenate([jnp.broadcast_to(lead[None], (b, BLOCK, D_MODEL)), x,
                         jnp.zeros((b, lp - length, D_MODEL), x.dtype)], axis=1)
    rc, rm, rp = _rope_tables(lp)

    rw_t = router_w.T.astype(F32)
    rw_hi = rw_t.astype(BF16)
    rw_lo = (rw_t - rw_hi.astype(F32)).astype(BF16)
    rw = jnp.stack([rw_hi, rw_lo])
    rb = router_bias.astype(F32).reshape(N_EXPERTS, 1)

    for l in range(DEPTH):
        lam_init = jnp.full((1, 1), 0.8 - 0.6 * math.exp(-0.3 * l), F32)
        lamp = jnp.stack([lambda_q1[l], lambda_k1[l], lambda_q2[l], lambda_k2[l]]).astype(F32)
        sq_t, sk, sv_t, dq_t, dk, dv_t, y_conv = _in_proj(
            h, w_in[l].astype(BF16), rc, rm, rp, conv_w[l].astype(F32))
        y_sb = _sb_attention(sq_t, sk, sv_t)
        y_diff = _diff_attention(dq_t, dk, dv_t, lamp, lam_init, diff_norm_g[l].astype(F32).reshape(HEAD_DIM, 1))
        wg = w_gate[l].transpose(1, 0, 2).reshape(D_MODEL, FF_W).astype(BF16)
        wu = w_up[l].transpose(1, 0, 2).reshape(D_MODEL, FF_W).astype(BF16)
        wd = w_down[l].reshape(FF_W, D_MODEL).astype(BF16)
        flat = lambda a: a.reshape(b * lp, a.shape[-1])
        h = _post(flat(y_sb), flat(y_diff), flat(y_conv), flat(h), w_out[l].astype(BF16),
                  ln1_g[l].reshape(1, D_MODEL), ln1_b[l].reshape(1, D_MODEL), rw, rb, wg, wu, wd,
                  ln2_g[l].reshape(1, D_MODEL), ln2_b[l].reshape(1, D_MODEL)).reshape(b, lp, D_MODEL)
    return h[:, BLOCK:length]
```

```python
import functools
import math

import jax
import jax.numpy as jnp
from jax import lax
from jax.experimental import pallas as pl
from jax.experimental.pallas import tpu as pltpu

D_MODEL = 1024
DEPTH = 4
N_META = 16
BLOCK = 128
PAD_FRONT = BLOCK - N_META
HEAD_DIM = 64
N_HEADS = 4
ATT_W = N_HEADS * HEAD_DIM
CONV_W = 512
IN_COLS = 6 * ATT_W + 3 * CONV_W
DIFF_SUB = HEAD_DIM // 2
ROPE_DIM = DIFF_SUB // 4
ROPE_THETA = 500000.0
CONV_K = 3
N_EXPERTS = 16
N_GROUPS = 4
EXPERTS_PER_GROUP = N_EXPERTS // N_GROUPS
D_FF_EXPERT = 128
FF_W = N_EXPERTS * D_FF_EXPERT
FF_CHUNK = 512
POST_PARTS = 2
ALPHA = (2.0 * DEPTH) ** 0.25
LN_EPS = 1e-5
NEG_BIG = -(2.0 ** 100)

TQ = 256
KB = 256
CONV_HALO = 8
SWEEP_UNROLL = 4

SUM_ROWS = 16
V_ROWS = HEAD_DIM + SUM_ROWS
LOG2E = math.log2(math.e)
SIGN_BIT = -2 ** 31
UNDERFLOW_LOG2 = 151.0

F32 = jnp.float32
BF16 = jnp.bfloat16
VMEM_LIMIT = 56 * 1024 * 1024


def _dot(a, b):
    return jnp.dot(a, b, preferred_element_type=F32)


def _dot_nt(a, b):
    return lax.dot_general(a, b, (((1,), (1,)), ((), ())), preferred_element_type=F32)


def _in_proj_kernel(h_ref, w_ref, rc_ref, rm_ref, rp_ref, cw_ref,
                    sqT_ref, sk_ref, svT_ref, dqT_ref, dk_ref, dvT_ref, yc_ref, ubuf_ref, *, tm):
    j = pl.program_id(1)
    x = h_ref[0].astype(BF16)

    def proj(c0, width):
        return _dot(x, w_ref[:, c0:c0 + width])

    def store_t(o_ref, val):
        for s in range(tm // TQ):
            o_ref[0, s] = val[s * TQ:(s + 1) * TQ, :].T.astype(o_ref.dtype)

    def rope(t):
        halves = []
        for half in range(ATT_W // 128):
            sl = slice(half * 128, (half + 1) * 128)
            th = t[:, sl]
            halves.append(th * rc_ref[:, sl]
                          + pltpu.roll(th, 128 - ROPE_DIM // 2, axis=1) * rm_ref[:, sl]
                          + pltpu.roll(th, ROPE_DIM // 2, axis=1) * rp_ref[:, sl])
        return jnp.concatenate(halves, axis=1)

    store_t(sqT_ref, proj(0, ATT_W) * (HEAD_DIM ** -0.5 * LOG2E))
    sk_ref[0] = proj(ATT_W, ATT_W).astype(BF16)
    store_t(svT_ref, proj(2 * ATT_W, ATT_W))
    store_t(dqT_ref, rope(proj(3 * ATT_W, ATT_W)) * (DIFF_SUB ** -0.5 * LOG2E))
    dk_ref[0] = rope(proj(4 * ATT_W, ATT_W)).astype(BF16)
    dv = proj(5 * ATT_W, ATT_W)
    ones = jnp.ones((V_ROWS - HEAD_DIM, TQ), BF16)
    for s in range(tm // TQ):
        dv_t = dv[s * TQ:(s + 1) * TQ, :].T.astype(BF16)
        for h in range(N_HEADS):
            dvT_ref[0, s, h * V_ROWS:h * V_ROWS + HEAD_DIM, :] = dv_t[h * HEAD_DIM:(h + 1) * HEAD_DIM, :]
            dvT_ref[0, s, h * V_ROWS + HEAD_DIM:(h + 1) * V_ROWS, :] = ones

    c0 = 6 * ATT_W
    cb = proj(c0, CONV_W)
    cc = proj(c0 + CONV_W, CONV_W)
    ch = proj(c0 + 2 * CONV_W, CONV_W)
    pos = j * tm + lax.broadcasted_iota(jnp.int32, (tm, CONV_W), 0)
    u = jnp.where(pos >= PAD_FRONT, cc * ch, 0.0)

    @pl.when(j == 0)
    def _():
        ubuf_ref[0:CONV_HALO, :] = jnp.zeros((CONV_HALO, CONV_W), F32)

    ubuf_ref[CONV_HALO:CONV_HALO + tm, :] = u
    u1 = ubuf_ref[CONV_HALO - 1:CONV_HALO - 1 + tm, :]
    u2 = ubuf_ref[CONV_HALO - 2:CONV_HALO - 2 + tm, :]
    conv = cw_ref[0:1, :] * u2 + cw_ref[1:2, :] * u1 + cw_ref[2:3, :] * u
    yc_ref[0] = (cb * conv).astype(BF16)
    ubuf_ref[0:CONV_HALO, :] = ubuf_ref[tm:tm + CONV_HALO, :]


def _in_proj(h, w, rc, rm, rp, cw):
    b, lp, _ = h.shape
    tm = 768 if lp % 768 == 0 else TQ
    nj = lp // tm
    nq = lp // TQ
    t_shape = jax.ShapeDtypeStruct((b, nq, ATT_W, TQ), BF16)
    n_shape = jax.ShapeDtypeStruct((b, lp, ATT_W), BF16)
    v_shape = jax.ShapeDtypeStruct((b, nq, N_HEADS * V_ROWS, TQ), BF16)
    t_spec = pl.BlockSpec((1, tm // TQ, ATT_W, TQ), lambda bi, j: (bi, j, 0, 0))
    v_spec = pl.BlockSpec((1, tm // TQ, N_HEADS * V_ROWS, TQ), lambda bi, j: (bi, j, 0, 0))
    n_spec = pl.BlockSpec((1, tm, ATT_W), lambda bi, j: (bi, j, 0))
    tab_spec = pl.BlockSpec((tm, ATT_W), lambda bi, j: (j, 0))
    return pl.pallas_call(
        functools.partial(_in_proj_kernel, tm=tm),
        grid=(b, nj),
        in_specs=[
            pl.BlockSpec((1, tm, D_MODEL), lambda bi, j: (bi, j, 0)),
            pl.BlockSpec((D_MODEL, IN_COLS), lambda bi, j: (0, 0)),
            tab_spec, tab_spec, tab_spec,
            pl.BlockSpec((CONV_K, CONV_W), lambda bi, j: (0, 0)),
        ],
        out_specs=[t_spec, n_spec, t_spec, t_spec, n_spec, v_spec,
                   pl.BlockSpec((1, tm, CONV_W), lambda bi, j: (bi, j, 0))],
        out_shape=[t_shape, n_shape, t_shape, t_shape, n_shape, v_shape,
                   jax.ShapeDtypeStruct((b, lp, CONV_W), BF16)],
        scratch_shapes=[pltpu.VMEM((tm + CONV_HALO, CONV_W), F32)],
        compiler_params=pltpu.CompilerParams(
            dimension_semantics=("arbitrary", "arbitrary"), vmem_limit_bytes=VMEM_LIMIT),
        name="in_proj",
    )(h, w, rc, rm, rp, cw)


def _store_masked_rows(qm_ref, q_t, width):
    frow = lax.broadcasted_iota(jnp.int32, q_t.shape, 0)
    zero = jnp.zeros_like(q_t)
    for i in range(ATT_W // width):
        qm_ref[i] = jnp.where((frow >= i * width) & (frow < (i + 1) * width), q_t, zero)


def _causal_sweep(qi, stage_a, stage_b, stage_ba, rest_is_zero=None):
    if rest_is_zero is None:
        _sweep_from(qi, "diag", stage_a, stage_b, stage_ba, None)
        return

    odd = qi & 1
    stage_a(qi, odd, "diag")

    @pl.when(qi == 0)
    def _():
        stage_b(0, 0)

    for parity in (0, 1):
        @pl.when((qi >= 1) & (odd == parity))
        def _(parity=parity):
            stage_ba(qi, parity, "first")

    @pl.when(qi >= 1)
    def _():
        stage_b(qi - 1, 1 - odd)

        @pl.when((qi >= 2) & jnp.logical_not(rest_is_zero()))
        def _():
            _sweep_from(qi - 2, "first", stage_a, stage_b, stage_ba, rest_is_zero)


def _sweep_from(top, top_kind, stage_a, stage_b, stage_ba, rest_is_zero):
    qi = top
    odd = qi & 1
    stage_a(qi, odd, top_kind)

    @pl.when((odd == 0) & (qi >= 2))
    def _():
        stage_ba(qi, 0, None)

    kb_odd = qi - 1 + odd
    n_pairs = jnp.maximum(kb_odd - 1, 0) // 2

    if rest_is_zero is None:
        def pair(kb):
            stage_ba(kb, 1, None)
            stage_ba(kb - 1, 0, None)

        def body(j, carry):
            for u in range(SWEEP_UNROLL):
                pair(kb_odd - 2 * (SWEEP_UNROLL * j + u))
            return carry

        n_trips = n_pairs // SWEEP_UNROLL
        lax.fori_loop(0, n_trips, body, 0)

        def tail(p, carry):
            pair(kb_odd - 2 * p)
            return carry

        lax.fori_loop(SWEEP_UNROLL * n_trips, n_pairs, tail, 0)

        @pl.when(qi >= 1)
        def _():
            stage_ba(1, 1, "first")

        stage_b(0, 0)
    else:
        def cond(carry):
            j, stop = carry
            return (j < n_pairs) & jnp.logical_not(stop)

        def body(carry):
            j, _ = carry
            kb = kb_odd - 2 * j
            stage_ba(kb, 1, None)

            @pl.when(jnp.logical_not(rest_is_zero()))
            def _():
                stage_ba(kb - 1, 0, None)

            return j + 1, rest_is_zero()

        _, stop = lax.while_loop(cond, body, (jnp.int32(0), jnp.zeros((), jnp.bool_)))

        @pl.when(jnp.logical_not(stop))
        def _():
            @pl.when(qi >= 1)
            def _():
                stage_ba(1, 1, "first")

            stage_b(0, 0)


def _key_mask(kind, qi, kb, inclusive):
    key_idx = kb * KB + lax.broadcasted_iota(jnp.int32, (KB, TQ), 0)
    valid = key_idx >= PAD_FRONT
    if kind == "diag":
        q_idx = qi * TQ + lax.broadcasted_iota(jnp.int32, (KB, TQ), 1)
        valid = valid & ((key_idx <= q_idx) if inclusive else (key_idx < q_idx))
    return valid


def _att_specs(b, lp, v_rows):
    nq = lp // TQ
    nkb = lp // KB
    in_specs = [
        pl.BlockSpec((1, 1, ATT_W, TQ), lambda bi, qi: (bi, qi, 0, 0)),
        pl.BlockSpec((1, lp, ATT_W), lambda bi, qi: (bi, 0, 0)),
        pl.BlockSpec((1, nkb, v_rows, KB), lambda bi, qi: (bi, 0, 0, 0)),
    ]
    out_spec = pl.BlockSpec((1, TQ, ATT_W), lambda bi, qi: (bi, qi, 0))
    out_shape = jax.ShapeDtypeStruct((b, lp, ATT_W), BF16)
    return (b, nq), in_specs, out_spec, out_shape


def _sb_kernel(qT_ref, k_ref, vT_ref, o_ref, qm_ref, upper_ref, lb_ref, lr_ref, acc_ref, later_ref):
    qi = pl.program_id(1)
    _store_masked_rows(qm_ref, qT_ref[0, 0], HEAD_DIM)
    @pl.when((pl.program_id(0) == 0) & (qi == 0))
    def _():
        si = lax.broadcasted_iota(jnp.int32, (KB, KB), 0)
        ji = lax.broadcasted_iota(jnp.int32, (KB, KB), 1)
        upper_ref[0:KB, :] = (ji > si).astype(BF16)
        upper_ref[KB:, :] = jnp.ones((SUM_ROWS, KB), BF16)

    acc_ref[...] = jnp.zeros_like(acc_ref)
    later_ref[...] = jnp.zeros_like(later_ref)

    def a_head(h, kb, slot, valid):
        kblk = k_ref[0, pl.ds(pl.multiple_of(kb * KB, KB), KB), :]
        z = _dot(kblk, qm_ref[h])
        if valid is not None:
            z = jnp.where(valid, z, NEG_BIG)
        neg_abs = lax.bitcast_convert_type(lax.bitcast_convert_type(z, jnp.int32) | SIGN_BIT, F32)
        sp = jnp.log(1.0 + jnp.exp2(neg_abs)) * LOG2E
        log_beta = jnp.minimum(z, 0.0) - sp
        lb_ref[slot * N_HEADS + h] = log_beta
        lr_ref[slot * N_HEADS + h] = (log_beta - z).astype(BF16)

    def b_intra(h, slot):
        return _dot(upper_ref[...], lr_ref[slot * N_HEADS + h])

    def b_value(h, kb, slot, sums):
        hs = slice(h * HEAD_DIM, (h + 1) * HEAD_DIM)
        w = jnp.exp2(lb_ref[slot * N_HEADS + h] + sums[0:KB, :])
        later = later_ref[h:h + 1, :]
        acc_ref[hs, :] += _dot(vT_ref[0, kb, hs, :], w.astype(BF16)) * jnp.exp2(later)
        later_ref[h:h + 1, :] = later + sums[KB:KB + 1, :]

    def stage_a(kb, slot, kind):
        valid = _key_mask(kind, qi, kb, inclusive=False)
        for h in range(N_HEADS):
            a_head(h, kb, slot, valid)

    def stage_b(kb, slot):
        intra = [b_intra(h, slot) for h in range(N_HEADS)]
        for h in range(N_HEADS):
            b_value(h, kb, slot, intra[h])

    def stage_ba(kb, slot, kind):
        valid = None if kind is None else _key_mask(kind, qi, kb - 1, inclusive=False)
        intra = {}
        for h in range(N_HEADS + 1):
            if h < N_HEADS:
                intra[h] = b_intra(h, slot)
                a_head(h, kb - 1, 1 - slot, valid)
            if h >= 1:
                b_value(h - 1, kb, slot, intra.pop(h - 1))

    def rest_is_zero():
        return jnp.max(later_ref[0:N_HEADS, :]) < -UNDERFLOW_LOG2

    _causal_sweep(qi, stage_a, stage_b, stage_ba, rest_is_zero)
    o_ref[0] = acc_ref[...].T.astype(o_ref.dtype)


def _sb_attention(q_t, k, v_t):
    b, lp, _ = k.shape
    grid, in_specs, out_spec, out_shape = _att_specs(b, lp, ATT_W)
    return pl.pallas_call(
        _sb_kernel,
        grid=grid,
        in_specs=in_specs,
        out_specs=out_spec,
        out_shape=out_shape,
        scratch_shapes=[pltpu.VMEM((N_HEADS, ATT_W, TQ), BF16), pltpu.VMEM((KB + SUM_ROWS, KB), BF16),
                        pltpu.VMEM((2 * N_HEADS, KB, TQ), F32), pltpu.VMEM((2 * N_HEADS, KB, TQ), BF16),
                        pltpu.VMEM((ATT_W, TQ), F32), pltpu.VMEM((8, TQ), F32)],
        compiler_params=pltpu.CompilerParams(
            dimension_semantics=("arbitrary", "arbitrary"), vmem_limit_bytes=VMEM_LIMIT),
        name="sb_attention",
    )(q_t, k, v_t)


def _diff_kernel(qT_ref, k_ref, vT_ref, lamp_ref, li_ref, g_ref, o_ref,
                 qm_ref, s_ref, bmax_ref, acc_ref, m_ref):
    qi = pl.program_id(1)
    _store_masked_rows(qm_ref, qT_ref[0, 0], DIFF_SUB)
    acc_ref[...] = jnp.zeros_like(acc_ref)
    m_ref[...] = jnp.full_like(m_ref, NEG_BIG)
    n_maps = 2 * N_HEADS

    def a_map(i, kb, slot, valid):
        kblk = k_ref[0, pl.ds(pl.multiple_of(kb * KB, KB), KB), :]
        s = _dot(kblk, qm_ref[i])
        if valid is not None:
            s = jnp.where(valid, s, NEG_BIG)
        s_ref[slot * n_maps + i] = s.astype(BF16)
        bmax_ref[slot, i:i + 1, :] = jnp.max(s, axis=0, keepdims=True)

    def b_map(i, kb, slot):
        vh = vT_ref[0, kb, (i // 2) * V_ROWS:(i // 2 + 1) * V_ROWS, :]
        m_old = m_ref[i:i + 1, :]
        m_new = jnp.maximum(m_old, bmax_ref[slot, i:i + 1, :]).astype(BF16)
        alpha = jnp.exp2(m_old - m_new.astype(F32))
        p = jnp.exp2(s_ref[slot * n_maps + i] - m_new)
        acc_ref[i] = alpha * acc_ref[i] + _dot(vh, p)
        m_ref[i:i + 1, :] = m_new.astype(F32)

    def stage_a(kb, slot, kind):
        valid = _key_mask(kind, qi, kb, inclusive=True)
        for i in range(n_maps):
            a_map(i, kb, slot, valid)

    def stage_b(kb, slot):
        for i in range(n_maps):
            b_map(i, kb, slot)

    def stage_ba(kb, slot, kind):
        valid = None if kind is None else _key_mask(kind, qi, kb - 1, inclusive=True)
        for i in range(n_maps):
            a_map(i, kb - 1, 1 - slot, valid)
            b_map(i, kb, slot)

    _causal_sweep(qi, stage_a, stage_b, stage_ba)

    lp = lamp_ref[...]
    lam_init = li_ref[...]
    lam = (jnp.exp(jnp.sum(lp[0:1] * lp[1:2], axis=1, keepdims=True))
           - jnp.exp(jnp.sum(lp[2:3] * lp[3:4], axis=1, keepdims=True)) + lam_init)
    outs = []
    for h in range(N_HEADS):
        a1 = acc_ref[2 * h]
        a2 = acc_ref[2 * h + 1]
        o = (a1[:HEAD_DIM] / a1[HEAD_DIM:HEAD_DIM + 1]
             - lam * (a2[:HEAD_DIM] / a2[HEAD_DIM:HEAD_DIM + 1]))
        ms = jnp.mean(o * o, axis=0, keepdims=True)
        outs.append(o * lax.rsqrt(ms + LN_EPS) * g_ref[...] * (1.0 - lam_init))
    o_ref[0] = jnp.concatenate(outs, axis=0).T.astype(o_ref.dtype)


def _diff_attention(q_t, k, v_t, lamp, lam_init, g):
    b, lp, _ = k.shape
    grid, in_specs, out_spec, out_shape = _att_specs(b, lp, N_HEADS * V_ROWS)
    in_specs = in_specs + [
        pl.BlockSpec((4, DIFF_SUB), lambda bi, qi: (0, 0)),
        pl.BlockSpec((1, 1), lambda bi, qi: (0, 0)),
        pl.BlockSpec((HEAD_DIM, 1), lambda bi, qi: (0, 0)),
    ]
    return pl.pallas_call(
        _diff_kernel,
        grid=grid,
        in_specs=in_specs,
        out_specs=out_spec,
        out_shape=out_shape,
        scratch_shapes=[pltpu.VMEM((2 * N_HEADS, ATT_W, TQ), BF16), pltpu.VMEM((4 * N_HEADS, KB, TQ), BF16),
                        pltpu.VMEM((2, 8, TQ), F32),
                        pltpu.VMEM((2 * N_HEADS, V_ROWS, TQ), F32), pltpu.VMEM((8, TQ), F32)],
        compiler_params=pltpu.CompilerParams(
            dimension_semantics=("arbitrary", "arbitrary"), vmem_limit_bytes=VMEM_LIMIT),
        name="diff_attention",
    )(q_t, k, v_t, lamp, lam_init, g)


N_SB_SCRATCH = 6


def _attention_kernel(sq_ref, sk_ref, sv_ref, dq_ref, dk_ref, dv_ref, lamp_ref, li_ref, g_ref,
                      osb_ref, odf_ref, *scratch):
    _sb_kernel(sq_ref, sk_ref, sv_ref, osb_ref, *scratch[:N_SB_SCRATCH])
    _diff_kernel(dq_ref, dk_ref, dv_ref, lamp_ref, li_ref, g_ref, odf_ref, *scratch[N_SB_SCRATCH:])


def _attention(sq_t, sk, sv_t, dq_t, dk, dv_t, lamp, lam_init, g):
    b, lp, _ = sk.shape
    grid, sb_specs, out_spec, out_shape = _att_specs(b, lp, ATT_W)
    _, df_specs, _, _ = _att_specs(b, lp, N_HEADS * V_ROWS)
    small = [pl.BlockSpec((4, DIFF_SUB), lambda bi, qi: (0, 0)),
             pl.BlockSpec((1, 1), lambda bi, qi: (0, 0)),
             pl.BlockSpec((HEAD_DIM, 1), lambda bi, qi: (0, 0))]
    return pl.pallas_call(
        _attention_kernel,
        grid=grid,
        in_specs=sb_specs + df_specs + small,
        out_specs=[out_spec, out_spec],
        out_shape=[out_shape, out_shape],
        scratch_shapes=[pltpu.VMEM((N_HEADS, ATT_W, TQ), BF16), pltpu.VMEM((KB + SUM_ROWS, KB), BF16),
                        pltpu.VMEM((2 * N_HEADS, KB, TQ), F32), pltpu.VMEM((2 * N_HEADS, KB, TQ), BF16),
                        pltpu.VMEM((ATT_W, TQ), F32), pltpu.VMEM((8, TQ), F32),
                        pltpu.VMEM((2 * N_HEADS, ATT_W, TQ), BF16), pltpu.VMEM((4 * N_HEADS, KB, TQ), BF16),
                        pltpu.VMEM((2, 8, TQ), F32),
                        pltpu.VMEM((2 * N_HEADS, V_ROWS, TQ), F32), pltpu.VMEM((8, TQ), F32)],
        compiler_params=pltpu.CompilerParams(
            dimension_semantics=("arbitrary", "arbitrary"), vmem_limit_bytes=VMEM_LIMIT),
        name="attention",
    )(sq_t, sk, sv_t, dq_t, dk, dv_t, lamp, lam_init, g)


def _layer_norm(x, g, b):
    mu = jnp.mean(x, axis=-1, keepdims=True)
    xc = x - mu
    var = jnp.mean(xc * xc, axis=-1, keepdims=True)
    return xc * lax.rsqrt(var + LN_EPS) * g + b


def _route(logits_t, bias_ref):
    lg = [logits_t[e:e + 1, :] for e in range(N_EXPERTS)]
    mx = functools.reduce(jnp.maximum, lg)
    ex = [jnp.exp(v - mx) for v in lg]
    den = functools.reduce(jnp.add, ex)
    probs = [v / den for v in ex]
    sel = [probs[e] + bias_ref[e:e + 1, :] for e in range(N_EXPERTS)]

    group_score = []
    for g in range(N_GROUPS):
        a = sel[g * EXPERTS_PER_GROUP:(g + 1) * EXPERTS_PER_GROUP]
        pair_sums = [a[i] + a[j] for i in range(EXPERTS_PER_GROUP) for j in range(i + 1, EXPERTS_PER_GROUP)]
        group_score.append(functools.reduce(jnp.maximum, pair_sums))
    best = group_score[0]
    gidx = jnp.zeros(best.shape, jnp.int32)
    for g in range(1, N_GROUPS):
        better = group_score[g] > best
        gidx = jnp.where(better, g, gidx)
        best = jnp.where(better, group_score[g], best)

    def in_group(vals):
        out = []
        for j in range(EXPERTS_PER_GROUP):
            v = vals[(N_GROUPS - 1) * EXPERTS_PER_GROUP + j]
            for g in range(N_GROUPS - 2, -1, -1):
                v = jnp.where(gidx == g, vals[g * EXPERTS_PER_GROUP + j], v)
            out.append(v)
        return out

    s_in = in_group(sel)
    p_in = in_group(probs)
    i1 = jnp.zeros(best.shape, jnp.int32)
    v1 = s_in[0]
    for j in range(1, EXPERTS_PER_GROUP):
        better = s_in[j] > v1
        i1 = jnp.where(better, j, i1)
        v1 = jnp.where(better, s_in[j], v1)
    i2 = jnp.full(best.shape, -1, jnp.int32)
    v2 = jnp.full(best.shape, -jnp.inf, F32)
    for j in range(EXPERTS_PER_GROUP):
        better = (i1 != j) & (s_in[j] > v2)
        i2 = jnp.where(better, j, i2)
        v2 = jnp.where(better, s_in[j], v2)
    w1 = functools.reduce(jnp.add, [jnp.where(i1 == j, p_in[j], 0.0) for j in range(EXPERTS_PER_GROUP)])
    w2 = functools.reduce(jnp.add, [jnp.where(i2 == j, p_in[j], 0.0) for j in range(EXPERTS_PER_GROUP)])
    wsum = w1 + w2
    w1 = w1 / wsum
    w2 = w2 / wsum
    gates = []
    for e in range(N_EXPERTS):
        g, j = divmod(e, EXPERTS_PER_GROUP)
        in_g = gidx == g
        gates.append(jnp.where(in_g & (i1 == j), w1, 0.0) + jnp.where(in_g & (i2 == j), w2, 0.0))
    return gates


def _post_kernel(ysb_ref, ydf_ref, ycv_ref, h_ref, wo_ref, g1_ref, b1_ref, rw_ref, rb_ref,
                 wg_ref, wu_ref, wd_ref, g2_ref, b2_ref, o_ref, gate_ref):
    @pl.when(pl.program_id(0) == 0)
    def _():
        gate_ref[...] = jnp.zeros_like(gate_ref)

    pm = h_ref.shape[0] // POST_PARTS
    parts = [slice(i * pm, (i + 1) * pm) for i in range(POST_PARTS)]
    n_chunks = FF_W // FF_CHUNK

    def out_proj(r):
        return (_dot(ysb_ref[r, :], wo_ref[0:ATT_W, :])
                + _dot(ydf_ref[r, :], wo_ref[ATT_W:2 * ATT_W, :])
                + _dot(ycv_ref[r, :], wo_ref[2 * ATT_W:, :]))

    def norm_route(i, mix):
        h1 = _layer_norm(ALPHA * h_ref[parts[i], :] + mix, g1_ref[...], b1_ref[...])
        x_hi = h1.astype(BF16)
        x_lo = (h1 - x_hi.astype(F32)).astype(BF16)
        logits_t = _dot_nt(rw_ref[0], x_hi) + _dot_nt(rw_ref[1], x_hi) + _dot_nt(rw_ref[0], x_lo)
        gates = _route(logits_t, rb_ref)
        for e in range(N_EXPERTS):
            gate_ref[i, e:e + 1, :] = gates[e]
        return h1, x_hi, gate_ref[i].T

    def expert_chunk(c, x_hi, gate_cols):
        cs = slice(c * FF_CHUNK, (c + 1) * FF_CHUNK)
        gate_act = _dot(x_hi, wg_ref[:, cs])
        up = _dot(x_hi, wu_ref[:, cs])
        hid = gate_act * (1.0 / (1.0 + jnp.exp(-gate_act))) * up
        parts = []
        for k in range(FF_CHUNK // D_FF_EXPERT):
            e = c * (FF_CHUNK // D_FF_EXPERT) + k
            parts.append(hid[:, k * D_FF_EXPERT:(k + 1) * D_FF_EXPERT] * gate_cols[:, e:e + 1])
        return _dot(jnp.concatenate(parts, axis=1).astype(BF16), wd_ref[cs, :])

    mix = [out_proj(r) for r in parts]
    routed = [norm_route(i, mix[i]) for i in range(POST_PARTS)]
    for i in range(POST_PARTS):
        h1, x_hi, gate_cols = routed[i]
        ffn = expert_chunk(0, x_hi, gate_cols)
        for c in range(1, n_chunks):
            ffn = ffn + expert_chunk(c, x_hi, gate_cols)
        o_ref[parts[i], :] = _layer_norm(ALPHA * h1 + ffn, g2_ref[...], b2_ref[...])


def _post(ysb, ydf, ycv, h, wo, g1, b1, rw, rb, wg, wu, wd, g2, b2):
    t = h.shape[0]
    tm = 1024 if t % 1024 == 0 else 512
    assert t % tm == 0, (t, tm)
    row = lambda w: pl.BlockSpec((tm, w), lambda i: (i, 0))
    full = lambda shape: pl.BlockSpec(shape, lambda i: (0,) * len(shape), pipeline_mode=pl.Buffered(1))
    return pl.pallas_call(
        _post_kernel,
        grid=(t // tm,),
        in_specs=[row(ATT_W), row(ATT_W), row(CONV_W), row(D_MODEL),
                  full((D_MODEL, D_MODEL)), full((1, D_MODEL)), full((1, D_MODEL)),
                  full((2, N_EXPERTS, D_MODEL)), full((N_EXPERTS, 1)),
                  full((D_MODEL, FF_W)), full((D_MODEL, FF_W)), full((FF_W, D_MODEL)),
                  full((1, D_MODEL)), full((1, D_MODEL))],
        out_specs=row(D_MODEL),
        out_shape=jax.ShapeDtypeStruct((t, D_MODEL), F32),
        scratch_shapes=[pltpu.VMEM((POST_PARTS, 128, tm // POST_PARTS), F32)],
        compiler_params=pltpu.CompilerParams(
            dimension_semantics=("arbitrary",), vmem_limit_bytes=VMEM_LIMIT),
        name="out_proj_moe",
    )(ysb, ydf, ycv, h, wo, g1, b1, rw, rb, wg, wu, wd, g2, b2)


def _rope_tables(lp):
    half = ROPE_DIM // 2
    pos = (jnp.arange(lp) - PAD_FRONT).astype(F32)
    inv = jnp.power(jnp.float32(ROPE_THETA), -jnp.arange(0, ROPE_DIM, 2, dtype=F32) / ROPE_DIM)
    ang = pos[:, None] * inv[None, :]
    cos, sin = jnp.cos(ang), jnp.sin(ang)
    pad = jnp.zeros((lp, DIFF_SUB - ROPE_DIM), F32)
    zero = jnp.zeros((lp, half), F32)
    rc = jnp.concatenate([cos, cos, pad + 1.0], axis=1)
    rm = jnp.concatenate([-sin, zero, pad], axis=1)
    rp = jnp.concatenate([zero, sin, pad], axis=1)
    reps = ATT_W // DIFF_SUB
    return tuple(jnp.tile(t, (1, reps)) for t in (rc, rm, rp))


def kernel(x, meta_tokens, w_in, conv_w, lambda_q1, lambda_k1, lambda_q2, lambda_k2, diff_norm_g, w_out,
           ln1_g, ln1_b, router_w, router_bias, w_gate, w_up, w_down, ln2_g, ln2_b):
    b, seq, _ = x.shape
    length = BLOCK + seq
    lp = -(-length // TQ) * TQ
    lead = jnp.concatenate([jnp.zeros((PAD_FRONT, D_MODEL), x.dtype), meta_tokens.astype(x.dtype)], axis=0)
    h = jnp.concatenate([jnp.broadcast_to(lead[None], (b, BLOCK, D_MODEL)), x,
                         jnp.zeros((b, lp - length, D_MODEL), x.dtype)], axis=1)
    rc, rm, rp = _rope_tables(lp)

    rw_t = router_w.T.astype(F32)
    rw_hi = rw_t.astype(BF16)
    rw_lo = (rw_t - rw_hi.astype(F32)).astype(BF16)
    rw = jnp.stack([rw_hi, rw_lo])
    rb = router_bias.astype(F32).reshape(N_EXPERTS, 1)

    for l in range(DEPTH):
        lam_init = jnp.full((1, 1), 0.8 - 0.6 * math.exp(-0.3 * l), F32)
        lamp = jnp.stack([lambda_q1[l], lambda_k1[l], lambda_q2[l], lambda_k2[l]]).astype(F32)
        sq_t, sk, sv_t, dq_t, dk, dv_t, y_conv = _in_proj(
            h, w_in[l].astype(BF16), rc, rm, rp, conv_w[l].astype(F32))
        y_sb, y_diff = _attention(sq_t, sk, sv_t, dq_t, dk, dv_t, lamp, lam_init,
                                  diff_norm_g[l].astype(F32).reshape(HEAD_DIM, 1))
        wg = w_gate[l].transpose(1, 0, 2).reshape(D_MODEL, FF_W).astype(BF16)
        wu = w_up[l].transpose(1, 0, 2).reshape(D_MODEL, FF_W).astype(BF16)
        wd = w_down[l].reshape(FF_W, D_MODEL).astype(BF16)
        flat = lambda a: a.reshape(b * lp, a.shape[-1])
        h = _post(flat(y_sb), flat(y_diff), flat(y_conv), flat(h), w_out[l].astype(BF16),
                  ln1_g[l].reshape(1, D_MODEL), ln1_b[l].reshape(1, D_MODEL), rw, rb, wg, wu, wd,
                  ln2_g[l].reshape(1, D_MODEL), ln2_b[l].reshape(1, D_MODEL)).reshape(b, lp, D_MODEL)
    return h[:, BLOCK:length]
```
